```python
import math
import jax, jax.numpy as jnp
from jax import lax
import numpy as np

D_MODEL = 1024
BATCH = 32
SEQ = 2048
DEPTH = 2
DEC_BATCH = 8
DEC_SEQ = 64
PAST_LEN = 1024

CHUNK = 64
D_A = D_MODEL // 2
CONV_A = 3
D_B = D_MODEL // 2
B_GROUPS = 4
B_GROUP_DIM = D_B // B_GROUPS
B_CHUNK = 128
D_C = D_MODEL // 2
CONV_C = 31
N_BRANCH = 3
N_MEM = 256
MEM_HEADS = 4
MEM_HEAD_DIM = D_MODEL // MEM_HEADS
MOE_GROUPS = 4
EXPERTS_PER_GROUP = 4
N_EXPERTS = MOE_GROUPS * EXPERTS_PER_GROUP
MOE_TOP_K = 2
D_EXPERT = D_MODEL // 4
EPS = 1e-6
COL_XA = 0
COL_BA = COL_XA + D_A
COL_CA = COL_BA + D_A
COL_UV = COL_CA + D_A
COL_GLU = COL_UV + 2 * D_B
COL_GATE = COL_GLU + 2 * D_C
D_IN = COL_GATE + N_BRANCH * D_MODEL

kernel_name = 'hybrid_stream_encoder_step'


def rmsnorm(x, g):
    xf = x.astype(jnp.float32)
    y = xf * lax.rsqrt(jnp.mean(xf * xf, axis=-1, keepdims=True) + EPS)
    return (y * g.astype(jnp.float32)).astype(x.dtype)


def layernorm(x, g, b):
    xf = x.astype(jnp.float32)
    xc = xf - jnp.mean(xf, axis=-1, keepdims=True)
    y = xc * lax.rsqrt(jnp.mean(xc * xc, axis=-1, keepdims=True) + EPS)
    return (y * g.astype(jnp.float32) + b.astype(jnp.float32)).astype(x.dtype)


def causal_dwconv(xp, w, b):
    y = lax.conv_general_dilated(xp, w[:, None, :].astype(xp.dtype), (1,), 'VALID',
                                 dimension_numbers=('NWC', 'WIO', 'NWC'),
                                 feature_group_count=xp.shape[-1])
    return y + b


def spatial_gate(u, v, w_s, b_s):
    n_b, L, _ = v.shape
    cl = min(L, B_CHUNK)
    n_c = L // cl
    vr = v.reshape(n_b, n_c, cl, B_GROUPS, B_GROUP_DIM)
    ws = jnp.tril(w_s[:, :cl, :cl]).astype(v.dtype)
    s = jnp.einsum('gij,bnjgc->bnigc', ws, vr) + b_s[:, :cl].T[None, None, :, :, None]
    return u * s.reshape(n_b, L, D_B)


def token_mix(h, buf_a, buf_c, w_in, conv_a_w, conv_a_b, w_a_out, ln_v_g, ln_v_b, w_s, b_s,
              w_b_out, conv_c_w, conv_c_b, ln_c_g, ln_c_b, w_c_out, w_o):
    n_b, L, _ = h.shape
    z = h @ w_in
    pa = jnp.concatenate([buf_a, z[..., COL_CA:COL_UV] * z[..., COL_XA:COL_BA]], axis=1)
    y_a = (z[..., COL_BA:COL_CA] * causal_dwconv(pa, conv_a_w, conv_a_b)) @ w_a_out
    uv = jax.nn.gelu(z[..., COL_UV:COL_GLU], approximate=False)
    u = uv[..., :D_B]
    v = layernorm(uv[..., D_B:], ln_v_g, ln_v_b)
    y_b = spatial_gate(u, v, w_s, b_s) @ w_b_out
    glu = z[..., COL_GLU:COL_GLU + D_C] * jax.nn.sigmoid(z[..., COL_GLU + D_C:COL_GATE])
    pc = jnp.concatenate([buf_c, glu], axis=1)
    y_c = jax.nn.silu(layernorm(causal_dwconv(pc, conv_c_w, conv_c_b), ln_c_g, ln_c_b)) @ w_c_out
    gates = jax.nn.sigmoid(z[..., COL_GATE:]).reshape(n_b, L, N_BRANCH, D_MODEL)
    merged = gates[..., 0, :] * y_a + gates[..., 1, :] * y_b + gates[..., 2, :] * y_c
    return merged @ w_o, pa[:, -(CONV_A - 1):], pc[:, -(CONV_C - 1):], v


def memory_kv(mem, g, w_k, w_v):
    n_b, m, _ = mem.shape
    mn = rmsnorm(mem, g)
    k = (mn @ w_k).reshape(n_b, m, MEM_HEADS, MEM_HEAD_DIM)
    v = (mn @ w_v).reshape(n_b, m, MEM_HEADS, MEM_HEAD_DIM)
    return k, v


def cross_attend(h, mem_k, mem_v, w_q, w_mo):
    n_b, L, _ = h.shape
    q = (h @ w_q).reshape(n_b, L, MEM_HEADS, MEM_HEAD_DIM)
    s = jnp.einsum('blhd,bmhd->bhlm', q, mem_k).astype(jnp.float32) * (MEM_HEAD_DIM ** -0.5)
    p = jax.nn.softmax(s, axis=-1).astype(mem_v.dtype)
    o = jnp.einsum('bhlm,bmhd->blhd', p, mem_v).reshape(n_b, L, D_MODEL)
    return o @ w_mo


def hier_moe(h, rg_w, rg_b, re_w, re_b, w1, w3, w2):
    n_b, L, _ = h.shape
    t = h.reshape(n_b * L, D_MODEL)
    p_group = jax.nn.softmax((t @ rg_w + rg_b).astype(jnp.float32), axis=-1)
    p_top, g_idx = lax.top_k(p_group, 1)
    le = (jnp.einsum('td,dge->tge', t, re_w) + re_b).astype(jnp.float32)
    le_sel = jnp.einsum('tge,tg->te', le, jax.nn.one_hot(g_idx[:, 0], MOE_GROUPS, dtype=jnp.float32))
    l_top, e_idx = lax.top_k(le_sel, MOE_TOP_K)
    w_sel = p_top * jax.nn.softmax(l_top, axis=-1)
    expert_id = g_idx * EXPERTS_PER_GROUP + e_idx
    gate = jnp.sum(jax.nn.one_hot(expert_id, N_EXPERTS, dtype=jnp.float32) * w_sel[..., None],
                   axis=1).astype(t.dtype)
    y = jnp.zeros_like(t)
    for e in range(N_EXPERTS):
        hid = jax.nn.silu(t @ w1[e]) * (t @ w3[e])
        y = y + gate[:, e:e + 1] * (hid @ w2[e])
    return y.reshape(n_b, L, D_MODEL)


def layer(x, buf_a, buf_c, mem_k, mem_v, norm1_g, mix_w, norm2_g, w_q, w_mo, norm3_g, moe_w):
    mix, new_a, new_c, v_rows = token_mix(rmsnorm(x, norm1_g), buf_a, buf_c, *mix_w)
    x = x + mix
    x = x + cross_attend(rmsnorm(x, norm2_g), mem_k, mem_v, w_q, w_mo)
    x = x + hier_moe(rmsnorm(x, norm3_g), *moe_w)
    return x, new_a, new_c, v_rows


def setup_inputs(seed: int = 0) -> dict:
    key = jax.random.key(seed)
    keys = list(jax.random.split(key, 48))

    def nrm(shape, scale=1.0):
        return scale * jax.random.normal(keys.pop(), shape, jnp.float32)

    def gain(shape):
        return 1.0 + nrm(shape, 0.1)

    return {
        'x_prompt': nrm((BATCH, SEQ, D_MODEL)),
        'x_sample': nrm((DEC_BATCH, DEC_SEQ, D_MODEL)),
        'state_conv_a': nrm((DEPTH, DEC_BATCH, CONV_A - 1, D_A), 0.5),
        'state_conv_c': nrm((DEPTH, DEC_BATCH, CONV_C - 1, D_C), 0.5),
        'cache_mem_k': nrm((DEPTH, DEC_BATCH, N_MEM, MEM_HEADS, MEM_HEAD_DIM)),
        'cache_mem_v': nrm((DEPTH, DEC_BATCH, N_MEM, MEM_HEADS, MEM_HEAD_DIM)),
        'mem_prompt': nrm((BATCH, N_MEM, D_MODEL)),
        'norm1_g': gain((DEPTH, D_MODEL)),
        'w_in': nrm((DEPTH, D_MODEL, D_IN), D_MODEL ** -0.5),
        'conv_a_w': nrm((DEPTH, CONV_A, D_A), CONV_A ** -0.5),
        'conv_a_b': nrm((DEPTH, D_A), 0.01),
        'w_a_out': nrm((DEPTH, D_A, D_MODEL), D_A ** -0.5),
        'ln_v_g': gain((DEPTH, D_B)),
        'ln_v_b': nrm((DEPTH, D_B), 0.01),
        'w_s': nrm((DEPTH, B_GROUPS, B_CHUNK, B_CHUNK), B_CHUNK ** -0.5),
        'b_s': gain((DEPTH, B_GROUPS, B_CHUNK)),
        'w_b_out': nrm((DEPTH, D_B, D_MODEL), D_B ** -0.5),
        'conv_c_w': nrm((DEPTH, CONV_C, D_C), CONV_C ** -0.5),
        'conv_c_b': nrm((DEPTH, D_C), 0.01),
        'ln_c_g': gain((DEPTH, D_C)),
        'ln_c_b': nrm((DEPTH, D_C), 0.01),
        'w_c_out': nrm((DEPTH, D_C, D_MODEL), D_C ** -0.5),
        'w_o': nrm((DEPTH, D_MODEL, D_MODEL), D_MODEL ** -0.5),
        'norm2_g': gain((DEPTH, D_MODEL)),
        'mem_norm_g': gain((DEPTH, D_MODEL)),
        'w_q': nrm((DEPTH, D_MODEL, D_MODEL), D_MODEL ** -0.5),
        'w_k': nrm((DEPTH, D_MODEL, D_MODEL), D_MODEL ** -0.5),
        'w_v': nrm((DEPTH, D_MODEL, D_MODEL), D_MODEL ** -0.5),
        'w_mo': nrm((DEPTH, D_MODEL, D_MODEL), D_MODEL ** -0.5),
        'norm3_g': gain((DEPTH, D_MODEL)),
        'rg_w': nrm((DEPTH, D_MODEL, MOE_GROUPS), D_MODEL ** -0.5),
        'rg_b': nrm((DEPTH, MOE_GROUPS), 0.01),
        're_w': nrm((DEPTH, D_MODEL, MOE_GROUPS, EXPERTS_PER_GROUP), D_MODEL ** -0.5),
        're_b': nrm((DEPTH, MOE_GROUPS, EXPERTS_PER_GROUP), 0.01),
        'w1': nrm((DEPTH, N_EXPERTS, D_MODEL, D_EXPERT), D_MODEL ** -0.5),
        'w3': nrm((DEPTH, N_EXPERTS, D_MODEL, D_EXPERT), D_MODEL ** -0.5),
        'w2': nrm((DEPTH, N_EXPERTS, D_EXPERT, D_MODEL), D_EXPERT ** -0.5),
        'final_norm_g': gain((D_MODEL,)),
    }


def reference(x_prompt, x_sample, state_conv_a, state_conv_c, cache_mem_k, cache_mem_v, mem_prompt,
              norm1_g, w_in, conv_a_w, conv_a_b, w_a_out, ln_v_g, ln_v_b, w_s, b_s, w_b_out,
              conv_c_w, conv_c_b, ln_c_g, ln_c_b, w_c_out, w_o, norm2_g, mem_norm_g, w_q, w_k, w_v,
              w_mo, norm3_g, rg_w, rg_b, re_w, re_b, w1, w3, w2, final_norm_g):
    x_p = x_prompt
    x_s = x_sample
    sa_p, sa_s, sc_p, sc_s, v_s, mk_p, mv_p = [], [], [], [], [], [], []
    for l in range(DEPTH):
        mix_w = (w_in[l], conv_a_w[l], conv_a_b[l], w_a_out[l], ln_v_g[l], ln_v_b[l], w_s[l], b_s[l],
                 w_b_out[l], conv_c_w[l], conv_c_b[l], ln_c_g[l], ln_c_b[l], w_c_out[l], w_o[l])
        moe_w = (rg_w[l], rg_b[l], re_w[l], re_b[l], w1[l], w3[l], w2[l])
        mk, mv = memory_kv(mem_prompt, mem_norm_g[l], w_k[l], w_v[l])
        zero_a = jnp.zeros((x_p.shape[0], CONV_A - 1, D_A), x_p.dtype)
        zero_c = jnp.zeros((x_p.shape[0], CONV_C - 1, D_C), x_p.dtype)
        x_p, na, nc, _ = layer(x_p, zero_a, zero_c, mk, mv, norm1_g[l], mix_w,
                               norm2_g[l], w_q[l], w_mo[l], norm3_g[l], moe_w)
        sa_p.append(na)
        sc_p.append(nc)
        mk_p.append(mk)
        mv_p.append(mv)
        x_s, na, nc, vr = layer(x_s, state_conv_a[l], state_conv_c[l], cache_mem_k[l], cache_mem_v[l],
                                norm1_g[l], mix_w, norm2_g[l], w_q[l], w_mo[l], norm3_g[l], moe_w)
        sa_s.append(na)
        sc_s.append(nc)
        v_s.append(vr)
    y_prompt = rmsnorm(x_p, final_norm_g)
    y_sample = rmsnorm(x_s, final_norm_g)
    return (y_prompt, y_sample, jnp.stack(sa_p), jnp.stack(sa_s), jnp.stack(sc_p), jnp.stack(sc_s),
            jnp.stack(v_s), jnp.stack(mk_p), jnp.stack(mv_p))
```

```python
import functools

import jax
import jax.numpy as jnp
from jax import lax
from jax.experimental import pallas as pl
from jax.experimental.pallas import tpu as pltpu

D_MODEL = 1024
D_A = D_MODEL // 2
CONV_A = 3
D_B = D_MODEL // 2
B_GROUPS = 4
B_GROUP_DIM = D_B // B_GROUPS
B_CHUNK = 128
D_C = D_MODEL // 2
CONV_C = 31
N_MEM = 256
MEM_HEADS = 4
MEM_HEAD_DIM = D_MODEL // MEM_HEADS
MOE_GROUPS = 4
EXPERTS_PER_GROUP = 4
N_EXPERTS = MOE_GROUPS * EXPERTS_PER_GROUP
D_EXPERT = D_MODEL // 4
EPS = 1e-6
COL_XA = 0
COL_BA = COL_XA + D_A
COL_CA = COL_BA + D_A
COL_UV = COL_CA + D_A
COL_GLU = COL_UV + 2 * D_B
COL_GATE = COL_GLU + 2 * D_C
D_IN = COL_GATE + 3 * D_MODEL

SUBLANES = 8
LANES = 128
VMEM_LIMIT_BYTES = 56 * 1024 * 1024

PA_NEW = SUBLANES
PA_HIST = PA_NEW - (CONV_A - 1)
PC_NEW = 32
PC_HIST = PC_NEW - (CONV_C - 1)
CONV_ROWS = 32
ROUTER_COLS = LANES
COL_EXPERT0 = MOE_GROUPS

BF16 = jnp.bfloat16
F32 = jnp.float32


def _dot(a, b):
    return jnp.dot(a, b, preferred_element_type=F32)


def _rmsnorm(x, g):
    ms = jnp.mean(x * x, axis=-1, keepdims=True)
    return x * lax.rsqrt(ms + EPS) * g


def _layernorm(x, g, b):
    xc = x - jnp.mean(x, axis=-1, keepdims=True)
    var = jnp.mean(xc * xc, axis=-1, keepdims=True)
    return xc * lax.rsqrt(var + EPS) * g + b


def _resident(shape):
    nd = len(shape)
    return pl.BlockSpec(shape, lambda *_: (0,) * nd, pipeline_mode=pl.Buffered(1))


def _mix_kernel(x_ref, sa_ref, sc_ref, g1_ref, win_ref, caw_ref, cab_ref, wa_ref, lvg_ref,
                lvb_ref, ws_ref, bs_ref, wb_ref, ccw_ref, ccb_ref, lcg_ref, lcb_ref, wc_ref,
                wo_ref, xo_ref, na_ref, nc_ref, v_ref, pa_buf, pc_buf, s_buf, act_buf, *,
                tl, cl):
    t = pl.program_id(1)

    @pl.when(t == 0)
    def _():
        pa_buf[PA_HIST:PA_NEW, :] = sa_ref[0]
        pc_buf[PC_HIST:PC_NEW, :] = sc_ref[0]

    x = x_ref[0]
    hb = _rmsnorm(x, g1_ref[...]).astype(BF16)

    def gate(i):
        lo = COL_GATE + i * D_MODEL
        return jax.nn.sigmoid(_dot(hb, win_ref[:, lo:lo + D_MODEL]))

    za = _dot(hb, win_ref[:, COL_XA:COL_UV])
    pa_buf[PA_NEW:PA_NEW + tl, :] = za[:, 2 * D_A:] * za[:, :D_A]
    conv_a = cab_ref[...]
    for k in range(CONV_A):
        conv_a = conv_a + caw_ref[k:k + 1, :] * pa_buf[PA_HIST + k:PA_HIST + k + tl, :]
    y_a = _dot((za[:, D_A:2 * D_A] * conv_a).astype(BF16), wa_ref[...])
    merged = gate(0) * y_a
    new_a = pa_buf[PA_HIST + tl:PA_NEW + tl, :]
    na_ref[0] = new_a
    pa_buf[PA_HIST:PA_NEW, :] = new_a

    zuv = _dot(hb, win_ref[:, COL_UV:COL_GLU])
    uv = 0.5 * zuv * (1.0 + lax.erf(zuv * (0.5 ** 0.5)))
    v = _layernorm(uv[:, D_B:], lvg_ref[...], lvb_ref[...])
    v_ref[0] = v
    vb = v.astype(BF16)
    row = lax.broadcasted_iota(jnp.int32, (cl, cl), 0)
    col = lax.broadcasted_iota(jnp.int32, (cl, cl), 1)
    for g in range(B_GROUPS):
        wg = jnp.where(row >= col, ws_ref[g], 0.0).astype(BF16)
        for c in range(tl // cl):
            blk = vb[c * cl:(c + 1) * cl, g * B_GROUP_DIM:(g + 1) * B_GROUP_DIM]
            s_buf[c * cl:(c + 1) * cl, g * B_GROUP_DIM:(g + 1) * B_GROUP_DIM] = (
                _dot(wg, blk) + bs_ref[:, g * B_GROUP_DIM:(g + 1) * B_GROUP_DIM])
    y_b = _dot((uv[:, :D_B] * s_buf[...]).astype(BF16), wb_ref[...])
    merged = merged + gate(1) * y_b

    zg = _dot(hb, win_ref[:, COL_GLU:COL_GATE])
    pc_buf[PC_NEW:PC_NEW + tl, :] = zg[:, :D_C] * jax.nn.sigmoid(zg[:, D_C:])
    for r in range(tl // CONV_ROWS):
        base = PC_HIST + r * CONV_ROWS
        acc = jnp.broadcast_to(ccb_ref[...], (CONV_ROWS, D_C))
        for k in range(CONV_C):
            acc = acc + ccw_ref[k:k + 1, :] * pc_buf[base + k:base + k + CONV_ROWS, :]
        ln = _layernorm(acc, lcg_ref[...], lcb_ref[...])
        act_buf[r * CONV_ROWS:(r + 1) * CONV_ROWS, :] = (ln * jax.nn.sigmoid(ln)).astype(BF16)
    y_c = _dot(act_buf[...], wc_ref[...])
    merged = merged + gate(2) * y_c
    new_c = pc_buf[PC_HIST + tl:PC_NEW + tl, :]
    nc_ref[0] = new_c
    pc_buf[PC_HIST:PC_NEW, :] = new_c

    xo_ref[0] = x + _dot(merged.astype(BF16), wo_ref[...])


def _token_mix(x, state_a, state_c, p, *, tl):
    n, L, _ = x.shape
    cl = min(L, B_CHUNK)
    assert L % tl == 0 and tl % cl == 0 and tl % CONV_ROWS == 0 and tl >= CONV_C - 1
    ws = p['w_s'][:, :cl, :cl]
    bs = jnp.repeat(p['b_s'][:, :cl].T, B_GROUP_DIM, axis=1)
    weights = [p['norm1_g'], p['w_in'], p['conv_a_w'], p['conv_a_b'], p['w_a_out'],
               p['ln_v_g'], p['ln_v_b'], ws, bs, p['w_b_out'], p['conv_c_w'], p['conv_c_b'],
               p['ln_c_g'], p['ln_c_b'], p['w_c_out'], p['w_o']]
    tile = lambda width: pl.BlockSpec((1, tl, width), lambda b, t: (b, t, 0))
    per_seq = lambda rows, width: pl.BlockSpec((1, rows, width), lambda b, t: (b, 0, 0))
    return pl.pallas_call(
        functools.partial(_mix_kernel, tl=tl, cl=cl),
        grid=(n, L // tl),
        in_specs=[tile(D_MODEL), per_seq(CONV_A - 1, D_A), per_seq(CONV_C - 1, D_C)]
        + [_resident(w.shape) for w in weights],
        out_specs=[tile(D_MODEL), per_seq(CONV_A - 1, D_A), per_seq(CONV_C - 1, D_C), tile(D_B)],
        out_shape=[jax.ShapeDtypeStruct((n, L, D_MODEL), F32),
                   jax.ShapeDtypeStruct((n, CONV_A - 1, D_A), F32),
                   jax.ShapeDtypeStruct((n, CONV_C - 1, D_C), F32),
                   jax.ShapeDtypeStruct((n, L, D_B), F32)],
        scratch_shapes=[pltpu.VMEM((PA_NEW + tl, D_A), F32),
                        pltpu.VMEM((PC_NEW + tl, D_C), F32),
                        pltpu.VMEM((tl, D_B), F32),
                        pltpu.VMEM((tl, D_C), BF16)],
        compiler_params=pltpu.CompilerParams(
            dimension_semantics=("parallel", "arbitrary"), vmem_limit_bytes=VMEM_LIMIT_BYTES),
        name="token_mix",
    )(x, state_a, state_c, *weights)


def _memkv_kernel(m_ref, g_ref, wk_ref, wv_ref, k_ref, v_ref):
    mb = _rmsnorm(m_ref[...], g_ref[...]).astype(BF16)
    k_ref[...] = _dot(mb, wk_ref[...])
    v_ref[...] = _dot(mb, wv_ref[...])


def _memory_kv(mem, p, *, tm):
    n, m, _ = mem.shape
    rows = n * m
    assert rows % tm == 0
    tile = pl.BlockSpec((tm, D_MODEL), lambda i: (i, 0))
    weights = [p['mem_norm_g'], p['w_k'], p['w_v']]
    k, v = pl.pallas_call(
        _memkv_kernel,
        grid=(rows // tm,),
        in_specs=[tile] + [_resident(w.shape) for w in weights],
        out_specs=[tile, tile],
        out_shape=[jax.ShapeDtypeStruct((rows, D_MODEL), F32)] * 2,
        compiler_params=pltpu.CompilerParams(
            dimension_semantics=("parallel",), vmem_limit_bytes=VMEM_LIMIT_BYTES),
        name="memory_kv",
    )(mem.reshape(rows, D_MODEL), *weights)
    return k.reshape(n, m, D_MODEL), v.reshape(n, m, D_MODEL)


def _attn_kernel(x_ref, k_ref, v_ref, g_ref, wq_ref, wmo_ref, xo_ref, o_buf):
    x = x_ref[0]
    hb = _rmsnorm(x, g_ref[...]).astype(BF16)
    qb = (_dot(hb, wq_ref[...]) * (MEM_HEAD_DIM ** -0.5)).astype(BF16)
    for h in range(MEM_HEADS):
        cols = slice(h * MEM_HEAD_DIM, (h + 1) * MEM_HEAD_DIM)
        kh = k_ref[0, :, cols].astype(BF16)
        vh = v_ref[0, :, cols].astype(BF16)
        s = lax.dot_general(qb[:, cols], kh, (((1,), (1,)), ((), ())),
                            preferred_element_type=F32)
        e = jnp.exp(s - jnp.max(s, axis=-1, keepdims=True))
        prob = e * (1.0 / jnp.sum(e, axis=-1, keepdims=True))
        o_buf[:, cols] = _dot(prob.astype(BF16), vh).astype(BF16)
    xo_ref[0] = x + _dot(o_buf[...], wmo_ref[...])


def _cross_attend(x, mem_k, mem_v, p, *, tl):
    n, L, _ = x.shape
    assert L % tl == 0
    tile = pl.BlockSpec((1, tl, D_MODEL), lambda b, t: (b, t, 0))
    per_seq = pl.BlockSpec((1, N_MEM, D_MODEL), lambda b, t: (b, 0, 0))
    weights = [p['norm2_g'], p['w_q'], p['w_mo']]
    return pl.pallas_call(
        _attn_kernel,
        grid=(n, L // tl),
        in_specs=[tile, per_seq, per_seq] + [_resident(w.shape) for w in weights],
        out_specs=tile,
        out_shape=jax.ShapeDtypeStruct((n, L, D_MODEL), F32),
        scratch_shapes=[pltpu.VMEM((tl, D_MODEL), BF16)],
        compiler_params=pltpu.CompilerParams(
            dimension_semantics=("parallel", "arbitrary"), vmem_limit_bytes=VMEM_LIMIT_BYTES),
        name="cross_attend",
    )(x, mem_k, mem_v, *weights)


def _expert_gates(logits):
    col = lax.broadcasted_iota(jnp.int32, logits.shape, 1)
    neg = -jnp.inf
    big = jnp.int32(ROUTER_COLS)
    lg = jnp.where(col < MOE_GROUPS, logits, neg)
    mg = jnp.max(lg, axis=-1, keepdims=True)
    p_top = 1.0 / jnp.sum(jnp.exp(lg - mg), axis=-1, keepdims=True)
    g_idx = jnp.min(jnp.where(lg == mg, col, big), axis=-1, keepdims=True)
    ecol = col - COL_EXPERT0
    in_group = (ecol >= 0) & (ecol < N_EXPERTS) & ((ecol >> 2) == g_idx)
    le = jnp.where(in_group, logits, neg)
    m1 = jnp.max(le, axis=-1, keepdims=True)
    i1 = jnp.min(jnp.where(le == m1, col, big), axis=-1, keepdims=True)
    le2 = jnp.where(col == i1, neg, le)
    m2 = jnp.max(le2, axis=-1, keepdims=True)
    i2 = jnp.min(jnp.where(le2 == m2, col, big), axis=-1, keepdims=True)
    e2 = jnp.exp(m2 - m1)
    w1 = p_top / (1.0 + e2)
    w2 = p_top * e2 / (1.0 + e2)
    return jnp.where(col == i1, w1, 0.0) + jnp.where(col == i2, w2, 0.0)


def _moe_kernel(x_ref, g_ref, wr_ref, br_ref, w1_ref, w3_ref, w2_ref, gf_ref, o_ref, *, final):
    x = x_ref[...]
    h = _rmsnorm(x, g_ref[...])
    logits = jnp.dot(h, wr_ref[...], preferred_element_type=F32,
                     precision=lax.Precision.HIGHEST) + br_ref[...]
    gates = _expert_gates(logits)
    hb = h.astype(BF16)
    acc = x
    for e in range(N_EXPERTS):
        a = _dot(hb, w1_ref[e])
        hid = a * jax.nn.sigmoid(a) * _dot(hb, w3_ref[e])
        g = gates[:, COL_EXPERT0 + e:COL_EXPERT0 + e + 1]
        acc = acc + _dot((hid * g).astype(BF16), w2_ref[e])
    o_ref[...] = _rmsnorm(acc, gf_ref[...]) if final else acc


def _hier_moe(x, p, final_g, *, tm, final):
    shape = x.shape
    t = x.reshape(-1, D_MODEL)
    rows = t.shape[0]
    assert rows % tm == 0
    tile = pl.BlockSpec((tm, D_MODEL), lambda i: (i, 0))
    weights = [p['norm3_g'], p['w_router'], p['b_router'], p['w1'], p['w3'], p['w2'], final_g]
    out = pl.pallas_call(
        functools.partial(_moe_kernel, final=final),
        grid=(rows // tm,),
        in_specs=[tile] + [_resident(w.shape) for w in weights],
        out_specs=tile,
        out_shape=jax.ShapeDtypeStruct((rows, D_MODEL), F32),
        compiler_params=pltpu.CompilerParams(
            dimension_semantics=("parallel",), vmem_limit_bytes=VMEM_LIMIT_BYTES),
        name="hier_moe",
    )(t, *weights)
    return out.reshape(shape)


def _layer_params(l, norm1_g, w_in, conv_a_w, conv_a_b, w_a_out, ln_v_g, ln_v_b, w_s, b_s,
                  w_b_out, conv_c_w, conv_c_b, ln_c_g, ln_c_b, w_c_out, w_o, norm2_g,
                  mem_norm_g, w_q, w_k, w_v, w_mo, norm3_g, rg_w, rg_b, re_w, re_b, w1, w3, w2):
    row = lambda a: a[l].reshape(1, -1)
    bf = lambda a: a[l].astype(BF16)
    w_router = jnp.concatenate([rg_w[l], re_w[l].reshape(D_MODEL, N_EXPERTS)], axis=1)
    b_router = jnp.concatenate([rg_b[l], re_b[l].reshape(N_EXPERTS)])
    pad = ROUTER_COLS - MOE_GROUPS - N_EXPERTS
    return dict(
        norm1_g=row(norm1_g), w_in=bf(w_in), conv_a_w=conv_a_w[l], conv_a_b=row(conv_a_b),
        w_a_out=bf(w_a_out), ln_v_g=row(ln_v_g), ln_v_b=row(ln_v_b), w_s=w_s[l], b_s=b_s[l],
        w_b_out=bf(w_b_out), conv_c_w=conv_c_w[l], conv_c_b=row(conv_c_b), ln_c_g=row(ln_c_g),
        ln_c_b=row(ln_c_b), w_c_out=bf(w_c_out), w_o=bf(w_o), norm2_g=row(norm2_g),
        mem_norm_g=row(mem_norm_g), w_q=bf(w_q), w_k=bf(w_k), w_v=bf(w_v), w_mo=bf(w_mo),
        norm3_g=row(norm3_g), w_router=jnp.pad(w_router, ((0, 0), (0, pad))),
        b_router=jnp.pad(b_router, (0, pad)).reshape(1, -1),
        w1=bf(w1), w3=bf(w3), w2=bf(w2))


def kernel(x_prompt, x_sample, state_conv_a, state_conv_c, cache_mem_k, cache_mem_v, mem_prompt, norm1_g, w_in, conv_a_w, conv_a_b, w_a_out, ln_v_g, ln_v_b, w_s, b_s, w_b_out, conv_c_w, conv_c_b, ln_c_g, ln_c_b, w_c_out, w_o, norm2_g, mem_norm_g, w_q, w_k, w_v, w_mo, norm3_g, rg_w, rg_b, re_w, re_b, w1, w3, w2, final_norm_g):
    depth = w_in.shape[0]
    n_p, n_s = x_prompt.shape[0], x_sample.shape[0]
    final_g = final_norm_g.reshape(1, -1)
    x_p, x_s = x_prompt, x_sample
    zero_a = jnp.zeros((n_p, CONV_A - 1, D_A), F32)
    zero_c = jnp.zeros((n_p, CONV_C - 1, D_C), F32)
    sa_p, sa_s, sc_p, sc_s, v_s, mk_p, mv_p = [], [], [], [], [], [], []
    for l in range(depth):
        p = _layer_params(l, norm1_g, w_in, conv_a_w, conv_a_b, w_a_out, ln_v_g, ln_v_b, w_s,
                          b_s, w_b_out, conv_c_w, conv_c_b, ln_c_g, ln_c_b, w_c_out, w_o,
                          norm2_g, mem_norm_g, w_q, w_k, w_v, w_mo, norm3_g, rg_w, rg_b, re_w,
                          re_b, w1, w3, w2)
        final = l == depth - 1
        mk, mv = _memory_kv(mem_prompt, p, tm=512)
        x_p, na, nc, _ = _token_mix(x_p, zero_a, zero_c, p, tl=512)
        x_p = _cross_attend(x_p, mk, mv, p, tl=512)
        x_p = _hier_moe(x_p, p, final_g, tm=512, final=final)
        sa_p.append(na)
        sc_p.append(nc)
        mk_p.append(mk.reshape(n_p, N_MEM, MEM_HEADS, MEM_HEAD_DIM))
        mv_p.append(mv.reshape(n_p, N_MEM, MEM_HEADS, MEM_HEAD_DIM))
        ck = cache_mem_k[l].reshape(n_s, N_MEM, D_MODEL)
        cv = cache_mem_v[l].reshape(n_s, N_MEM, D_MODEL)
        x_s, na, nc, vr = _token_mix(x_s, state_conv_a[l], state_conv_c[l], p, tl=x_s.shape[1])
        x_s = _cross_attend(x_s, ck, cv, p, tl=x_s.shape[1])
        x_s = _hier_moe(x_s, p, final_g, tm=x_s.shape[0] * x_s.shape[1], final=final)
        sa_s.append(na)
        sc_s.append(nc)
        v_s.append(vr)
    return (x_p, x_s, jnp.stack(sa_p), jnp.stack(sa_s), jnp.stack(sc_p), jnp.stack(sc_s),
            jnp.stack(v_s), jnp.stack(mk_p), jnp.stack(mv_p))
```

```python
import functools

import jax
import jax.numpy as jnp
from jax import lax
from jax.experimental import pallas as pl
from jax.experimental.pallas import tpu as pltpu

D_MODEL = 1024
D_A = D_MODEL // 2
CONV_A = 3
D_B = D_MODEL // 2
B_GROUPS = 4
B_GROUP_DIM = D_B // B_GROUPS
B_CHUNK = 128
D_C = D_MODEL // 2
CONV_C = 31
N_MEM = 256
MEM_HEADS = 4
MEM_HEAD_DIM = D_MODEL // MEM_HEADS
MOE_GROUPS = 4
EXPERTS_PER_GROUP = 4
N_EXPERTS = MOE_GROUPS * EXPERTS_PER_GROUP
D_EXPERT = D_MODEL // 4
EPS = 1e-6
COL_XA = 0
COL_BA = COL_XA + D_A
COL_CA = COL_BA + D_A
COL_UV = COL_CA + D_A
COL_GLU = COL_UV + 2 * D_B
COL_GATE = COL_GLU + 2 * D_C
D_IN = COL_GATE + 3 * D_MODEL

SUBLANES = 8
LANES = 128
VMEM_LIMIT_BYTES = 56 * 1024 * 1024

PA_NEW = SUBLANES
PA_HIST = PA_NEW - (CONV_A - 1)
PC_NEW = 32
PC_HIST = PC_NEW - (CONV_C - 1)
CONV_ROWS = 32
ROUTER_COLS = LANES
COL_EXPERT0 = MOE_GROUPS

BF16 = jnp.bfloat16
F32 = jnp.float32


def _dot(a, b):
    return jnp.dot(a, b, preferred_element_type=F32)


def _rmsnorm(x, g):
    ms = jnp.mean(x * x, axis=-1, keepdims=True)
    return x * lax.rsqrt(ms + EPS) * g


def _layernorm(x, g, b):
    xc = x - jnp.mean(x, axis=-1, keepdims=True)
    var = jnp.mean(xc * xc, axis=-1, keepdims=True)
    return xc * lax.rsqrt(var + EPS) * g + b


def _resident(shape):
    nd = len(shape)
    return pl.BlockSpec(shape, lambda *_: (0,) * nd, pipeline_mode=pl.Buffered(1))


def _mix_kernel(x_ref, sa_ref, sc_ref, g1_ref, win_ref, caw_ref, cab_ref, wa_ref, lvg_ref,
                lvb_ref, ws_ref, bs_ref, wb_ref, ccw_ref, ccb_ref, lcg_ref, lcb_ref, wc_ref,
                wo_ref, xo_ref, na_ref, nc_ref, *rest, tl, cl, emit_v):
    v_ref = rest[0] if emit_v else None
    pa_buf, pc_buf, s_buf, act_buf = rest[-4:]
    t = pl.program_id(1)

    @pl.when(t == 0)
    def _():
        pa_buf[PA_HIST:PA_NEW, :] = sa_ref[0]
        pc_buf[PC_HIST:PC_NEW, :] = sc_ref[0]

    x = x_ref[0]
    hb = _rmsnorm(x, g1_ref[...]).astype(BF16)

    def gate(i):
        lo = COL_GATE + i * D_MODEL
        return jax.nn.sigmoid(_dot(hb, win_ref[:, lo:lo + D_MODEL]))

    zg = _dot(hb, win_ref[:, COL_GLU:COL_GATE])
    pc_buf[PC_NEW:PC_NEW + tl, :] = zg[:, :D_C] * jax.nn.sigmoid(zg[:, D_C:])
    for r in range(tl // CONV_ROWS):
        base = PC_HIST + r * CONV_ROWS
        acc = jnp.broadcast_to(ccb_ref[...], (CONV_ROWS, D_C))
        for off in range(SUBLANES):
            part = None
            for k in range(CONV_C):
                if (base + k) % SUBLANES == off:
                    term = ccw_ref[k:k + 1, :] * pc_buf[base + k:base + k + CONV_ROWS, :]
                    part = term if part is None else part + term
            acc = acc + part
        ln = _layernorm(acc, lcg_ref[...], lcb_ref[...])
        act_buf[r * CONV_ROWS:(r + 1) * CONV_ROWS, :] = (ln * jax.nn.sigmoid(ln)).astype(BF16)
    new_c = pc_buf[PC_HIST + tl:PC_NEW + tl, :]
    nc_ref[0] = new_c
    pc_buf[PC_HIST:PC_NEW, :] = new_c

    za = _dot(hb, win_ref[:, COL_XA:COL_UV])
    pa_buf[PA_NEW:PA_NEW + tl, :] = za[:, 2 * D_A:] * za[:, :D_A]
    conv_a = cab_ref[...]
    for k in range(CONV_A):
        conv_a = conv_a + caw_ref[k:k + 1, :] * pa_buf[PA_HIST + k:PA_HIST + k + tl, :]
    y_a = _dot((za[:, D_A:2 * D_A] * conv_a).astype(BF16), wa_ref[...])
    merged = gate(0) * y_a
    new_a = pa_buf[PA_HIST + tl:PA_NEW + tl, :]
    na_ref[0] = new_a
    pa_buf[PA_HIST:PA_NEW, :] = new_a

    zuv = _dot(hb, win_ref[:, COL_UV:COL_GLU])
    uv = 0.5 * zuv * (1.0 + lax.erf(zuv * (0.5 ** 0.5)))
    v = _layernorm(uv[:, D_B:], lvg_ref[...], lvb_ref[...])
    if emit_v:
        v_ref[0] = v
    vb = v.astype(BF16)
    row = lax.broadcasted_iota(jnp.int32, (cl, cl), 0)
    col = lax.broadcasted_iota(jnp.int32, (cl, cl), 1)
    for g in range(B_GROUPS):
        wg = jnp.where(row >= col, ws_ref[g], 0.0).astype(BF16)
        for c in range(tl // cl):
            blk = vb[c * cl:(c + 1) * cl, g * B_GROUP_DIM:(g + 1) * B_GROUP_DIM]
            s_buf[c * cl:(c + 1) * cl, g * B_GROUP_DIM:(g + 1) * B_GROUP_DIM] = (
                _dot(wg, blk) + bs_ref[:, g * B_GROUP_DIM:(g + 1) * B_GROUP_DIM])
    y_b = _dot((uv[:, :D_B] * s_buf[...]).astype(BF16), wb_ref[...])
    merged = merged + gate(1) * y_b

    y_c = _dot(act_buf[...], wc_ref[...])
    merged = merged + gate(2) * y_c

    xo_ref[0] = x + _dot(merged.astype(BF16), wo_ref[...])


def _token_mix(x, state_a, state_c, p, *, tl, emit_v):
    n, L, _ = x.shape
    cl = min(L, B_CHUNK)
    assert L % tl == 0 and tl % cl == 0 and tl % CONV_ROWS == 0 and tl >= CONV_C - 1
    ws = p['w_s'][:, :cl, :cl]
    bs = jnp.repeat(p['b_s'][:, :cl].T, B_GROUP_DIM, axis=1)
    weights = [p['norm1_g'], p['w_in'], p['conv_a_w'], p['conv_a_b'], p['w_a_out'],
               p['ln_v_g'], p['ln_v_b'], ws, bs, p['w_b_out'], p['conv_c_w'], p['conv_c_b'],
               p['ln_c_g'], p['ln_c_b'], p['w_c_out'], p['w_o']]
    tile = lambda width: pl.BlockSpec((1, tl, width), lambda b, t: (b, t, 0))
    per_seq = lambda rows, width: pl.BlockSpec((1, rows, width), lambda b, t: (b, 0, 0))
    return pl.pallas_call(
        functools.partial(_mix_kernel, tl=tl, cl=cl, emit_v=emit_v),
        grid=(n, L // tl),
        in_specs=[tile(D_MODEL), per_seq(CONV_A - 1, D_A), per_seq(CONV_C - 1, D_C)]
        + [_resident(w.shape) for w in weights],
        out_specs=[tile(D_MODEL), per_seq(CONV_A - 1, D_A), per_seq(CONV_C - 1, D_C)]
        + ([tile(D_B)] if emit_v else []),
        out_shape=[jax.ShapeDtypeStruct((n, L, D_MODEL), F32),
                   jax.ShapeDtypeStruct((n, CONV_A - 1, D_A), F32),
                   jax.ShapeDtypeStruct((n, CONV_C - 1, D_C), F32)]
        + ([jax.ShapeDtypeStruct((n, L, D_B), F32)] if emit_v else []),
        scratch_shapes=[pltpu.VMEM((PA_NEW + tl, D_A), F32),
                        pltpu.VMEM((PC_NEW + tl, D_C), F32),
                        pltpu.VMEM((tl, D_B), F32),
                        pltpu.VMEM((tl, D_C), BF16)],
        compiler_params=pltpu.CompilerParams(
            dimension_semantics=("parallel", "arbitrary"), vmem_limit_bytes=VMEM_LIMIT_BYTES),
        name="token_mix",
    )(x, state_a, state_c, *weights)


def _memkv_kernel(m_ref, g_ref, wk_ref, wv_ref, k_ref, v_ref):
    mb = _rmsnorm(m_ref[...], g_ref[...]).astype(BF16)
    k_ref[...] = _dot(mb, wk_ref[...])
    v_ref[...] = _dot(mb, wv_ref[...])


def _memory_kv(mem, p, *, tm):
    n, m, _ = mem.shape
    rows = n * m
    assert rows % tm == 0
    tile = pl.BlockSpec((tm, D_MODEL), lambda i: (i, 0))
    weights = [p['mem_norm_g'], p['w_k'], p['w_v']]
    k, v = pl.pallas_call(
        _memkv_kernel,
        grid=(rows // tm,),
        in_specs=[tile] + [_resident(w.shape) for w in weights],
        out_specs=[tile, tile],
        out_shape=[jax.ShapeDtypeStruct((rows, D_MODEL), F32)] * 2,
        compiler_params=pltpu.CompilerParams(
            dimension_semantics=("parallel",), vmem_limit_bytes=VMEM_LIMIT_BYTES),
        name="memory_kv",
    )(mem.reshape(rows, D_MODEL), *weights)
    return k.reshape(n, m, D_MODEL), v.reshape(n, m, D_MODEL)


def _attn_kernel(x_ref, k_ref, v_ref, g_ref, wq_ref, wmo_ref, xo_ref, o_buf):
    x = x_ref[0]
    hb = _rmsnorm(x, g_ref[...]).astype(BF16)
    qb = (_dot(hb, wq_ref[...]) * (MEM_HEAD_DIM ** -0.5)).astype(BF16)
    for h in range(MEM_HEADS):
        cols = slice(h * MEM_HEAD_DIM, (h + 1) * MEM_HEAD_DIM)
        kh = k_ref[0, :, cols].astype(BF16)
        vh = v_ref[0, :, cols].astype(BF16)
        s = lax.dot_general(qb[:, cols], kh, (((1,), (1,)), ((), ())),
                            preferred_element_type=F32)
        e = jnp.exp(s - jnp.max(s, axis=-1, keepdims=True))
        prob = e * (1.0 / jnp.sum(e, axis=-1, keepdims=True))
        o_buf[:, cols] = _dot(prob.astype(BF16), vh).astype(BF16)
    xo_ref[0] = x + _dot(o_buf[...], wmo_ref[...])


def _cross_attend(x, mem_k, mem_v, p, *, tl):
    n, L, _ = x.shape
    assert L % tl == 0
    tile = pl.BlockSpec((1, tl, D_MODEL), lambda b, t: (b, t, 0))
    per_seq = pl.BlockSpec((1, N_MEM, D_MODEL), lambda b, t: (b, 0, 0))
    weights = [p['norm2_g'], p['w_q'], p['w_mo']]
    return pl.pallas_call(
        _attn_kernel,
        grid=(n, L // tl),
        in_specs=[tile, per_seq, per_seq] + [_resident(w.shape) for w in weights],
        out_specs=tile,
        out_shape=jax.ShapeDtypeStruct((n, L, D_MODEL), F32),
        scratch_shapes=[pltpu.VMEM((tl, D_MODEL), BF16)],
        compiler_params=pltpu.CompilerParams(
            dimension_semantics=("parallel", "arbitrary"), vmem_limit_bytes=VMEM_LIMIT_BYTES),
        name="cross_attend",
    )(x, mem_k, mem_v, *weights)


def _expert_gates(logits):
    col = lax.broadcasted_iota(jnp.int32, logits.shape, 1)
    neg = -jnp.inf
    big = jnp.int32(ROUTER_COLS)
    lg = jnp.where(col < MOE_GROUPS, logits, neg)
    mg = jnp.max(lg, axis=-1, keepdims=True)
    p_top = 1.0 / jnp.sum(jnp.exp(lg - mg), axis=-1, keepdims=True)
    g_idx = jnp.min(jnp.where(lg == mg, col, big), axis=-1, keepdims=True)
    ecol = col - COL_EXPERT0
    in_group = (ecol >= 0) & (ecol < N_EXPERTS) & ((ecol >> 2) == g_idx)
    le = jnp.where(in_group, logits, neg)
    m1 = jnp.max(le, axis=-1, keepdims=True)
    i1 = jnp.min(jnp.where(le == m1, col, big), axis=-1, keepdims=True)
    le2 = jnp.where(col == i1, neg, le)
    m2 = jnp.max(le2, axis=-1, keepdims=True)
    i2 = jnp.min(jnp.where(le2 == m2, col, big), axis=-1, keepdims=True)
    e2 = jnp.exp(m2 - m1)
    w1 = p_top / (1.0 + e2)
    w2 = p_top * e2 / (1.0 + e2)
    return jnp.where(col == i1, w1, 0.0) + jnp.where(col == i2, w2, 0.0)


def _moe_kernel(x_ref, g_ref, wr_ref, br_ref, w1_ref, w3_ref, w2_ref, gf_ref, o_ref, *, final):
    x = x_ref[...]
    h = _rmsnorm(x, g_ref[...])
    logits = jnp.dot(h, wr_ref[...], preferred_element_type=F32,
                     precision=lax.Precision.HIGHEST) + br_ref[...]
    gates = _expert_gates(logits)
    hb = h.astype(BF16)
    acc = x
    for e in range(N_EXPERTS):
        a = _dot(hb, w1_ref[e])
        hid = a * jax.nn.sigmoid(a) * _dot(hb, w3_ref[e])
        g = gates[:, COL_EXPERT0 + e:COL_EXPERT0 + e + 1]
        acc = acc + _dot((hid * g).astype(BF16), w2_ref[e])
    o_ref[...] = _rmsnorm(acc, gf_ref[...]) if final else acc


def _hier_moe(x, p, final_g, *, tm, final):
    shape = x.shape
    t = x.reshape(-1, D_MODEL)
    rows = t.shape[0]
    assert rows % tm == 0
    tile = pl.BlockSpec((tm, D_MODEL), lambda i: (i, 0))
    weights = [p['norm3_g'], p['w_router'], p['b_router'], p['w1'], p['w3'], p['w2'], final_g]
    out = pl.pallas_call(
        functools.partial(_moe_kernel, final=final),
        grid=(rows // tm,),
        in_specs=[tile] + [_resident(w.shape) for w in weights],
        out_specs=tile,
        out_shape=jax.ShapeDtypeStruct((rows, D_MODEL), F32),
        compiler_params=pltpu.CompilerParams(
            dimension_semantics=("parallel",), vmem_limit_bytes=VMEM_LIMIT_BYTES),
        name="hier_moe",
    )(t, *weights)
    return out.reshape(shape)


def _layer_params(l, norm1_g, w_in, conv_a_w, conv_a_b, w_a_out, ln_v_g, ln_v_b, w_s, b_s,
                  w_b_out, conv_c_w, conv_c_b, ln_c_g, ln_c_b, w_c_out, w_o, norm2_g,
                  mem_norm_g, w_q, w_k, w_v, w_mo, norm3_g, rg_w, rg_b, re_w, re_b, w1, w3, w2):
    row = lambda a: a[l].reshape(1, -1)
    bf = lambda a: a[l].astype(BF16)
    w_router = jnp.concatenate([rg_w[l], re_w[l].reshape(D_MODEL, N_EXPERTS)], axis=1)
    b_router = jnp.concatenate([rg_b[l], re_b[l].reshape(N_EXPERTS)])
    pad = ROUTER_COLS - MOE_GROUPS - N_EXPERTS
    return dict(
        norm1_g=row(norm1_g), w_in=bf(w_in), conv_a_w=conv_a_w[l], conv_a_b=row(conv_a_b),
        w_a_out=bf(w_a_out), ln_v_g=row(ln_v_g), ln_v_b=row(ln_v_b), w_s=w_s[l], b_s=b_s[l],
        w_b_out=bf(w_b_out), conv_c_w=conv_c_w[l], conv_c_b=row(conv_c_b), ln_c_g=row(ln_c_g),
        ln_c_b=row(ln_c_b), w_c_out=bf(w_c_out), w_o=bf(w_o), norm2_g=row(norm2_g),
        mem_norm_g=row(mem_norm_g), w_q=bf(w_q), w_k=bf(w_k), w_v=bf(w_v), w_mo=bf(w_mo),
        norm3_g=row(norm3_g), w_router=jnp.pad(w_router, ((0, 0), (0, pad))),
        b_router=jnp.pad(b_router, (0, pad)).reshape(1, -1),
        w1=bf(w1), w3=bf(w3), w2=bf(w2))


def kernel(x_prompt, x_sample, state_conv_a, state_conv_c, cache_mem_k, cache_mem_v, mem_prompt, norm1_g, w_in, conv_a_w, conv_a_b, w_a_out, ln_v_g, ln_v_b, w_s, b_s, w_b_out, conv_c_w, conv_c_b, ln_c_g, ln_c_b, w_c_out, w_o, norm2_g, mem_norm_g, w_q, w_k, w_v, w_mo, norm3_g, rg_w, rg_b, re_w, re_b, w1, w3, w2, final_norm_g):
    depth = w_in.shape[0]
    n_p, n_s = x_prompt.shape[0], x_sample.shape[0]
    final_g = final_norm_g.reshape(1, -1)
    x_p, x_s = x_prompt, x_sample
    zero_a = jnp.zeros((n_p, CONV_A - 1, D_A), F32)
    zero_c = jnp.zeros((n_p, CONV_C - 1, D_C), F32)
    sa_p, sa_s, sc_p, sc_s, v_s, mk_p, mv_p = [], [], [], [], [], [], []
    for l in range(depth):
        p = _layer_params(l, norm1_g, w_in, conv_a_w, conv_a_b, w_a_out, ln_v_g, ln_v_b, w_s,
                          b_s, w_b_out, conv_c_w, conv_c_b, ln_c_g, ln_c_b, w_c_out, w_o,
                          norm2_g, mem_norm_g, w_q, w_k, w_v, w_mo, norm3_g, rg_w, rg_b, re_w,
                          re_b, w1, w3, w2)
        final = l == depth - 1
        mk, mv = _memory_kv(mem_prompt, p, tm=512)
        x_p, na, nc = _token_mix(x_p, zero_a, zero_c, p, tl=512, emit_v=False)
        x_p = _cross_attend(x_p, mk, mv, p, tl=512)
        x_p = _hier_moe(x_p, p, final_g, tm=512, final=final)
        sa_p.append(na)
        sc_p.append(nc)
        mk_p.append(mk.reshape(n_p, N_MEM, MEM_HEADS, MEM_HEAD_DIM))
        mv_p.append(mv.reshape(n_p, N_MEM, MEM_HEADS, MEM_HEAD_DIM))
        ck = cache_mem_k[l].reshape(n_s, N_MEM, D_MODEL)
        cv = cache_mem_v[l].reshape(n_s, N_MEM, D_MODEL)
        x_s, na, nc, vr = _token_mix(x_s, state_conv_a[l], state_conv_c[l], p, tl=x_s.shape[1],
                                     emit_v=True)
        x_s = _cross_attend(x_s, ck, cv, p, tl=x_s.shape[1])
        x_s = _hier_moe(x_s, p, final_g, tm=x_s.shape[0] * x_s.shape[1], final=final)
        sa_s.append(na)
        sc_s.append(nc)
        v_s.append(vr)
    return (x_p, x_s, jnp.stack(sa_p), jnp.stack(sa_s), jnp.stack(sc_p), jnp.stack(sc_s),
            jnp.stack(v_s), jnp.stack(mk_p), jnp.stack(mv_p))
```

```python
import functools

import jax
import jax.numpy as jnp
from jax import lax
from jax.experimental import pallas as pl
from jax.experimental.pallas import tpu as pltpu

D_MODEL = 1024
D_A = D_MODEL // 2
CONV_A = 3
D_B = D_MODEL // 2
B_GROUPS = 4
B_GROUP_DIM = D_B // B_GROUPS
B_CHUNK = 128
D_C = D_MODEL // 2
CONV_C = 31
N_MEM = 256
MEM_HEADS = 4
MEM_HEAD_DIM = D_MODEL // MEM_HEADS
MOE_GROUPS = 4
EXPERTS_PER_GROUP = 4
N_EXPERTS = MOE_GROUPS * EXPERTS_PER_GROUP
D_EXPERT = D_MODEL // 4
EPS = 1e-6
COL_XA = 0
COL_BA = COL_XA + D_A
COL_CA = COL_BA + D_A
COL_UV = COL_CA + D_A
COL_GLU = COL_UV + 2 * D_B
COL_GATE = COL_GLU + 2 * D_C
D_IN = COL_GATE + 3 * D_MODEL

SUBLANES = 8
LANES = 128
VMEM_LIMIT_BYTES = 56 * 1024 * 1024

PA_NEW = SUBLANES
PA_HIST = PA_NEW - (CONV_A - 1)
PC_NEW = 32
PC_HIST = PC_NEW - (CONV_C - 1)
CONV_ROWS = 32
ROUTER_COLS = LANES
COL_EXPERT0 = MOE_GROUPS
PAIRS_PER_GROUP = EXPERTS_PER_GROUP * (EXPERTS_PER_GROUP - 1) // 2
N_CLASSES = MOE_GROUPS * PAIRS_PER_GROUP
INFO_CLASS, INFO_RANK, INFO_W_LO, INFO_W_HI = 0, 1, 2, 3
ROUTED_WIDTH = D_MODEL + ROUTER_COLS

BF16 = jnp.bfloat16
F32 = jnp.float32


def _dot(a, b):
    return jnp.dot(a, b, preferred_element_type=F32)


def _rmsnorm(x, g):
    ms = jnp.mean(x * x, axis=-1, keepdims=True)
    return x * lax.rsqrt(ms + EPS) * g


def _layernorm(x, g, b):
    xc = x - jnp.mean(x, axis=-1, keepdims=True)
    var = jnp.mean(xc * xc, axis=-1, keepdims=True)
    return xc * lax.rsqrt(var + EPS) * g + b


def _resident(shape):
    nd = len(shape)
    return pl.BlockSpec(shape, lambda *_: (0,) * nd, pipeline_mode=pl.Buffered(1))


def _mix_kernel(x_ref, sa_ref, sc_ref, g1_ref, win_ref, caw_ref, cab_ref, wa_ref, lvg_ref,
                lvb_ref, ws_ref, bs_ref, wb_ref, ccw_ref, ccb_ref, lcg_ref, lcb_ref, wc_ref,
                wo_ref, xo_ref, na_ref, nc_ref, *rest, tl, cl, emit_v):
    v_ref = rest[0] if emit_v else None
    pa_buf, pc_buf, s_buf, act_buf = rest[-4:]
    t = pl.program_id(1)

    @pl.when(t == 0)
    def _():
        pa_buf[PA_HIST:PA_NEW, :] = sa_ref[0]
        pc_buf[PC_HIST:PC_NEW, :] = sc_ref[0]

    x = x_ref[0]
    hb = _rmsnorm(x, g1_ref[...]).astype(BF16)

    def gate(i):
        lo = COL_GATE + i * D_MODEL
        return jax.nn.sigmoid(_dot(hb, win_ref[:, lo:lo + D_MODEL]))

    zg = _dot(hb, win_ref[:, COL_GLU:COL_GATE])
    pc_buf[PC_NEW:PC_NEW + tl, :] = zg[:, :D_C] * jax.nn.sigmoid(zg[:, D_C:])
    for r in range(tl // CONV_ROWS):
        base = PC_HIST + r * CONV_ROWS
        acc = jnp.broadcast_to(ccb_ref[...], (CONV_ROWS, D_C))
        for off in range(SUBLANES):
            part = None
            for k in range(CONV_C):
                if (base + k) % SUBLANES == off:
                    term = ccw_ref[k:k + 1, :] * pc_buf[base + k:base + k + CONV_ROWS, :]
                    part = term if part is None else part + term
            acc = acc + part
        ln = _layernorm(acc, lcg_ref[...], lcb_ref[...])
        act_buf[r * CONV_ROWS:(r + 1) * CONV_ROWS, :] = (ln * jax.nn.sigmoid(ln)).astype(BF16)
    new_c = pc_buf[PC_HIST + tl:PC_NEW + tl, :]
    nc_ref[0] = new_c
    pc_buf[PC_HIST:PC_NEW, :] = new_c

    za = _dot(hb, win_ref[:, COL_XA:COL_UV])
    pa_buf[PA_NEW:PA_NEW + tl, :] = za[:, 2 * D_A:] * za[:, :D_A]
    conv_a = cab_ref[...]
    for k in range(CONV_A):
        conv_a = conv_a + caw_ref[k:k + 1, :] * pa_buf[PA_HIST + k:PA_HIST + k + tl, :]
    y_a = _dot((za[:, D_A:2 * D_A] * conv_a).astype(BF16), wa_ref[...])
    merged = gate(0) * y_a
    new_a = pa_buf[PA_HIST + tl:PA_NEW + tl, :]
    na_ref[0] = new_a
    pa_buf[PA_HIST:PA_NEW, :] = new_a

    zuv = _dot(hb, win_ref[:, COL_UV:COL_GLU])
    uv = 0.5 * zuv * (1.0 + lax.erf(zuv * (0.5 ** 0.5)))
    v = _layernorm(uv[:, D_B:], lvg_ref[...], lvb_ref[...])
    if emit_v:
        v_ref[0] = v
    vb = v.astype(BF16)
    row = lax.broadcasted_iota(jnp.int32, (cl, cl), 0)
    col = lax.broadcasted_iota(jnp.int32, (cl, cl), 1)
    for g in range(B_GROUPS):
        wg = jnp.where(row >= col, ws_ref[g], 0.0).astype(BF16)
        for c in range(tl // cl):
            blk = vb[c * cl:(c + 1) * cl, g * B_GROUP_DIM:(g + 1) * B_GROUP_DIM]
            s_buf[c * cl:(c + 1) * cl, g * B_GROUP_DIM:(g + 1) * B_GROUP_DIM] = (
                _dot(wg, blk) + bs_ref[:, g * B_GROUP_DIM:(g + 1) * B_GROUP_DIM])
    y_b = _dot((uv[:, :D_B] * s_buf[...]).astype(BF16), wb_ref[...])
    merged = merged + gate(1) * y_b

    y_c = _dot(act_buf[...], wc_ref[...])
    merged = merged + gate(2) * y_c

    xo_ref[0] = x + _dot(merged.astype(BF16), wo_ref[...])


def _token_mix(x, state_a, state_c, p, *, tl, emit_v):
    n, L, _ = x.shape
    cl = min(L, B_CHUNK)
    assert L % tl == 0 and tl % cl == 0 and tl % CONV_ROWS == 0 and tl >= CONV_C - 1
    ws = p['w_s'][:, :cl, :cl]
    bs = jnp.repeat(p['b_s'][:, :cl].T, B_GROUP_DIM, axis=1)
    weights = [p['norm1_g'], p['w_in'], p['conv_a_w'], p['conv_a_b'], p['w_a_out'],
               p['ln_v_g'], p['ln_v_b'], ws, bs, p['w_b_out'], p['conv_c_w'], p['conv_c_b'],
               p['ln_c_g'], p['ln_c_b'], p['w_c_out'], p['w_o']]
    tile = lambda width: pl.BlockSpec((1, tl, width), lambda b, t: (b, t, 0))
    per_seq = lambda rows, width: pl.BlockSpec((1, rows, width), lambda b, t: (b, 0, 0))
    return pl.pallas_call(
        functools.partial(_mix_kernel, tl=tl, cl=cl, emit_v=emit_v),
        grid=(n, L // tl),
        in_specs=[tile(D_MODEL), per_seq(CONV_A - 1, D_A), per_seq(CONV_C - 1, D_C)]
        + [_resident(w.shape) for w in weights],
        out_specs=[tile(D_MODEL), per_seq(CONV_A - 1, D_A), per_seq(CONV_C - 1, D_C)]
        + ([tile(D_B)] if emit_v else []),
        out_shape=[jax.ShapeDtypeStruct((n, L, D_MODEL), F32),
                   jax.ShapeDtypeStruct((n, CONV_A - 1, D_A), F32),
                   jax.ShapeDtypeStruct((n, CONV_C - 1, D_C), F32)]
        + ([jax.ShapeDtypeStruct((n, L, D_B), F32)] if emit_v else []),
        scratch_shapes=[pltpu.VMEM((PA_NEW + tl, D_A), F32),
                        pltpu.VMEM((PC_NEW + tl, D_C), F32),
                        pltpu.VMEM((tl, D_B), F32),
                        pltpu.VMEM((tl, D_C), BF16)],
        compiler_params=pltpu.CompilerParams(
            dimension_semantics=("parallel", "arbitrary"), vmem_limit_bytes=VMEM_LIMIT_BYTES),
        name="token_mix",
    )(x, state_a, state_c, *weights)


def _memkv_kernel(m_ref, g_ref, wk_ref, wv_ref, k_ref, v_ref):
    mb = _rmsnorm(m_ref[...], g_ref[...]).astype(BF16)
    k_ref[...] = _dot(mb, wk_ref[...])
    v_ref[...] = _dot(mb, wv_ref[...])


def _memory_kv(mem, p, *, tm):
    n, m, _ = mem.shape
    rows = n * m
    assert rows % tm == 0
    tile = pl.BlockSpec((tm, D_MODEL), lambda i: (i, 0))
    weights = [p['mem_norm_g'], p['w_k'], p['w_v']]
    k, v = pl.pallas_call(
        _memkv_kernel,
        grid=(rows // tm,),
        in_specs=[tile] + [_resident(w.shape) for w in weights],
        out_specs=[tile, tile],
        out_shape=[jax.ShapeDtypeStruct((rows, D_MODEL), F32)] * 2,
        compiler_params=pltpu.CompilerParams(
            dimension_semantics=("parallel",), vmem_limit_bytes=VMEM_LIMIT_BYTES),
        name="memory_kv",
    )(mem.reshape(rows, D_MODEL), *weights)
    return k.reshape(n, m, D_MODEL), v.reshape(n, m, D_MODEL)


def _attn_kernel(x_ref, k_ref, v_ref, g_ref, wq_ref, wmo_ref, *rest, route):
    xo_ref, o_buf = rest[-2:]
    x = x_ref[0]
    hb = _rmsnorm(x, g_ref[...]).astype(BF16)
    qb = (_dot(hb, wq_ref[...]) * (MEM_HEAD_DIM ** -0.5)).astype(BF16)
    for h in range(MEM_HEADS):
        cols = slice(h * MEM_HEAD_DIM, (h + 1) * MEM_HEAD_DIM)
        kh = k_ref[0, :, cols].astype(BF16)
        vh = v_ref[0, :, cols].astype(BF16)
        s = lax.dot_general(qb[:, cols], kh, (((1,), (1,)), ((), ())),
                            preferred_element_type=F32)
        e = jnp.exp(s - jnp.max(s, axis=-1, keepdims=True))
        prob = e * (1.0 / jnp.sum(e, axis=-1, keepdims=True))
        o_buf[:, cols] = _dot(prob.astype(BF16), vh).astype(BF16)
    xn = x + _dot(o_buf[...], wmo_ref[...])
    if route:
        g3_ref, wr_ref, br_ref = rest[:3]
        xo_ref[0, :, :D_MODEL] = xn
        xo_ref[0, :, D_MODEL:] = _route_info(_router_logits(xn, g3_ref, wr_ref, br_ref))
    else:
        xo_ref[0] = xn


def _cross_attend(x, mem_k, mem_v, p, *, tl, route):
    n, L, _ = x.shape
    assert L % tl == 0
    width = D_MODEL + ROUTER_COLS if route else D_MODEL
    tile = pl.BlockSpec((1, tl, D_MODEL), lambda b, t: (b, t, 0))
    per_seq = pl.BlockSpec((1, N_MEM, D_MODEL), lambda b, t: (b, 0, 0))
    weights = [p['norm2_g'], p['w_q'], p['w_mo']]
    if route:
        weights += [p['norm3_g'], p['w_router'], p['b_router']]
    return pl.pallas_call(
        functools.partial(_attn_kernel, route=route),
        grid=(n, L // tl),
        in_specs=[tile, per_seq, per_seq] + [_resident(w.shape) for w in weights],
        out_specs=pl.BlockSpec((1, tl, width), lambda b, t: (b, t, 0)),
        out_shape=jax.ShapeDtypeStruct((n, L, width), F32),
        scratch_shapes=[pltpu.VMEM((tl, D_MODEL), BF16)],
        compiler_params=pltpu.CompilerParams(
            dimension_semantics=("parallel", "arbitrary"), vmem_limit_bytes=VMEM_LIMIT_BYTES),
        name="cross_attend",
    )(x, mem_k, mem_v, *weights)


def _router_logits(x, g_ref, wr_ref, br_ref):
    h = _rmsnorm(x, g_ref[...])
    h_hi = h.astype(BF16)
    h_lo = (h - h_hi.astype(F32)).astype(BF16)
    by_hi = _dot(h_hi, wr_ref[...])
    by_lo = _dot(h_lo, wr_ref[:, :ROUTER_COLS])
    return by_hi[:, :ROUTER_COLS] + (by_hi[:, ROUTER_COLS:] + by_lo) + br_ref[...]


def _route(logits):
    col = lax.broadcasted_iota(jnp.int32, logits.shape, 1)
    neg = -jnp.inf
    big = jnp.int32(ROUTER_COLS)
    lg = jnp.where(col < MOE_GROUPS, logits, neg)
    mg = jnp.max(lg, axis=-1, keepdims=True)
    p_top = 1.0 / jnp.sum(jnp.exp(lg - mg), axis=-1, keepdims=True)
    g_idx = jnp.min(jnp.where(lg == mg, col, big), axis=-1, keepdims=True)
    ecol = col - COL_EXPERT0
    in_group = (ecol >= 0) & (ecol < N_EXPERTS) & ((ecol >> 2) == g_idx)
    le = jnp.where(in_group, logits, neg)
    m1 = jnp.max(le, axis=-1, keepdims=True)
    i1 = jnp.min(jnp.where(le == m1, col, big), axis=-1, keepdims=True)
    le2 = jnp.where(col == i1, neg, le)
    m2 = jnp.max(le2, axis=-1, keepdims=True)
    i2 = jnp.min(jnp.where(le2 == m2, col, big), axis=-1, keepdims=True)
    e2 = jnp.exp(m2 - m1)
    w1 = p_top / (1.0 + e2)
    w2 = p_top * e2 / (1.0 + e2)
    return col, g_idx, i1, i2, w1, w2


def _expert_gates(logits):
    col, _, i1, i2, w1, w2 = _route(logits)
    return jnp.where(col == i1, w1, 0.0) + jnp.where(col == i2, w2, 0.0)


def _route_info(logits):
    tm = logits.shape[0]
    col, g_idx, i1, i2, w1, w2 = _route(logits)
    first_is_lo = i1 < i2
    a = (jnp.minimum(i1, i2) - COL_EXPERT0) & (EXPERTS_PER_GROUP - 1)
    b = (jnp.maximum(i1, i2) - COL_EXPERT0) & (EXPERTS_PER_GROUP - 1)
    cls = g_idx * PAIRS_PER_GROUP + ((a * (2 * EXPERTS_PER_GROUP - 1 - a)) >> 1) + (b - a - 1)
    onehot = jnp.where(col == cls, 1.0, 0.0)
    row_i = lax.broadcasted_iota(jnp.int32, (tm, tm), 0)
    col_i = lax.broadcasted_iota(jnp.int32, (tm, tm), 1)
    tri = jnp.where(row_i >= col_i, 1.0, 0.0).astype(BF16)
    running = _dot(tri, onehot.astype(BF16))
    rank = jnp.sum(running * onehot, axis=-1, keepdims=True) - 1.0
    w_lo = jnp.where(first_is_lo, w1, w2)
    w_hi = jnp.where(first_is_lo, w2, w1)
    return jnp.where(col == INFO_CLASS, cls.astype(F32),
                     jnp.where(col == INFO_RANK, rank,
                               jnp.where(col == INFO_W_LO, w_lo,
                                         jnp.where(col == INFO_W_HI, w_hi, 0.0))))


def _moe_kernel(x_ref, g_ref, wr_ref, br_ref, w1_ref, w3_ref, w2_ref, gf_ref, o_ref, *, final):
    x = x_ref[...]
    gates = _expert_gates(_router_logits(x, g_ref, wr_ref, br_ref))
    hb = _rmsnorm(x, g_ref[...]).astype(BF16)
    acc = x
    for e in range(N_EXPERTS):
        a = _dot(hb, w1_ref[e])
        hid = a * jax.nn.sigmoid(a) * _dot(hb, w3_ref[e])
        g = gates[:, COL_EXPERT0 + e:COL_EXPERT0 + e + 1]
        acc = acc + _dot((hid * g).astype(BF16), w2_ref[e])
    o_ref[...] = _rmsnorm(acc, gf_ref[...]) if final else acc


def _hier_moe(x, p, final_g, *, tm, final):
    shape = x.shape
    t = x.reshape(-1, D_MODEL)
    rows = t.shape[0]
    assert rows % tm == 0
    tile = pl.BlockSpec((tm, D_MODEL), lambda i: (i, 0))
    weights = [p['norm3_g'], p['w_router'], p['b_router'], p['w1'], p['w3'], p['w2'], final_g]
    out = pl.pallas_call(
        functools.partial(_moe_kernel, final=final),
        grid=(rows // tm,),
        in_specs=[tile] + [_resident(w.shape) for w in weights],
        out_specs=tile,
        out_shape=jax.ShapeDtypeStruct((rows, D_MODEL), F32),
        compiler_params=pltpu.CompilerParams(
            dimension_semantics=("parallel",), vmem_limit_bytes=VMEM_LIMIT_BYTES),
        name="hier_moe",
    )(t, *weights)
    return out.reshape(shape)


def _permute_kernel(pos_ref, src_ref, dst_ref, sem, *, rows, gather):
    def row_copy(j):
        p = pos_ref[0, 0, j]
        if gather:
            return pltpu.make_async_copy(src_ref.at[pl.ds(p, 1)], dst_ref.at[pl.ds(j, 1)], sem)
        return pltpu.make_async_copy(src_ref.at[pl.ds(j, 1)], dst_ref.at[pl.ds(p, 1)], sem)

    for j in range(rows):
        row_copy(j).start()

    def wait(j, carry):
        row_copy(0).wait()
        return carry

    lax.fori_loop(0, rows, wait, 0, unroll=8)


def _permute_rows(src, pos, *, rows, gather):
    n, width = src.shape
    assert n % rows == 0
    steps = n // rows
    tile = pl.BlockSpec((rows, width), lambda s: (s, 0))
    hbm = pl.BlockSpec(memory_space=pl.ANY)
    return pl.pallas_call(
        functools.partial(_permute_kernel, rows=rows, gather=gather),
        grid=(steps,),
        in_specs=[pl.BlockSpec((1, 1, rows), lambda s: (s, 0, 0), memory_space=pltpu.SMEM),
                  hbm if gather else tile],
        out_specs=tile if gather else hbm,
        out_shape=jax.ShapeDtypeStruct((n, width), src.dtype),
        scratch_shapes=[pltpu.SemaphoreType.DMA(())],
        compiler_params=pltpu.CompilerParams(dimension_semantics=("arbitrary",)),
        name="gather_rows" if gather else "scatter_rows",
    )(pos.reshape(steps, 1, rows), src)


def _routing_plan(cls, rank, *, rank_tile, tm):
    n = cls.shape[0]
    tiles = n // tm
    rank_tiles = n // rank_tile
    onehot = cls.reshape(rank_tiles, rank_tile, 1) == jnp.arange(N_CLASSES, dtype=jnp.int32)
    counts = jnp.sum(onehot.astype(jnp.int32), axis=1)
    class_end = jnp.cumsum(jnp.sum(counts, axis=0))
    class_start = class_end - jnp.sum(counts, axis=0)
    tile_base = class_start[None, :] + jnp.cumsum(counts, axis=0) - counts
    tile_of = jnp.arange(n, dtype=jnp.int32) // rank_tile
    pos = tile_base.reshape(-1)[tile_of * N_CLASSES + cls] + rank
    cuts = jnp.sort(jnp.concatenate([jnp.arange(tiles, dtype=jnp.int32) * tm, class_start[1:]]))
    lo = cuts
    hi = jnp.concatenate([cuts[1:], jnp.array([n], jnp.int32)])
    tile = jnp.minimum(lo // tm, tiles - 1)
    c = jnp.minimum(jnp.searchsorted(class_end, lo, side='right'), N_CLASSES - 1)
    first = ((hi > lo) & (lo == tile * tm)).astype(jnp.int32)
    group, pair = c // PAIRS_PER_GROUP, c % PAIRS_PER_GROUP
    pair_lo = jnp.array([a for a in range(EXPERTS_PER_GROUP)
                         for _ in range(a + 1, EXPERTS_PER_GROUP)], jnp.int32)
    pair_hi = jnp.array([b for a in range(EXPERTS_PER_GROUP)
                         for b in range(a + 1, EXPERTS_PER_GROUP)], jnp.int32)
    e_lo = group * EXPERTS_PER_GROUP + pair_lo[pair]
    e_hi = group * EXPERTS_PER_GROUP + pair_hi[pair]
    as_i32 = lambda a: a.astype(jnp.int32)
    return as_i32(pos), tuple(as_i32(a) for a in (tile, e_lo, e_hi, lo, hi, first))


def _moe_visit_kernel(tile_ref, elo_ref, ehi_ref, lo_ref, hi_ref, first_ref, xr_ref, g_ref,
                      w1a_ref, w1b_ref, w3a_ref, w3b_ref, w2a_ref, w2b_ref, gf_ref, o_ref, *,
                      tm, final):
    v = pl.program_id(0)
    lo, hi = lo_ref[v], hi_ref[v]

    @pl.when(hi > lo)
    def _():
        x = xr_ref[:, :D_MODEL]
        info = xr_ref[:, D_MODEL:]
        hb = _rmsnorm(x, g_ref[...]).astype(BF16)

        def hidden(w1_ref, w3_ref, gate):
            a = _dot(hb, w1_ref[0])
            return (a * jax.nn.sigmoid(a) * _dot(hb, w3_ref[0]) * gate).astype(BF16)

        h_lo = hidden(w1a_ref, w3a_ref, info[:, INFO_W_LO:INFO_W_LO + 1])
        h_hi = hidden(w1b_ref, w3b_ref, info[:, INFO_W_HI:INFO_W_HI + 1])
        res = x + (_dot(h_lo, w2a_ref[0]) + _dot(h_hi, w2b_ref[0]))
        if final:
            res = _rmsnorm(res, gf_ref[...])
        row = tile_ref[v] * tm + lax.broadcasted_iota(jnp.int32, (tm, 1), 0)
        mine = (row >= lo) & (row < hi)

        @pl.when(first_ref[v] == 1)
        def _():
            o_ref[...] = jnp.where(mine, res, 0.0)

        @pl.when(first_ref[v] == 0)
        def _():
            o_ref[...] = jnp.where(mine, res, o_ref[...])


def _moe_visits(xr_sorted, visits, p, final_g, *, tm, final):
    n = xr_sorted.shape[0]
    n_visits = visits[0].shape[0]
    by_tile = lambda width: pl.BlockSpec((tm, width), lambda v, tile, *_: (tile[v], 0))
    expert_lo = lambda shape: pl.BlockSpec((1,) + shape, lambda v, tile, elo, *_: (elo[v], 0, 0))
    expert_hi = lambda shape: pl.BlockSpec((1,) + shape,
                                           lambda v, tile, elo, ehi, *_: (ehi[v], 0, 0))
    up, down = (D_MODEL, D_EXPERT), (D_EXPERT, D_MODEL)
    return pl.pallas_call(
        functools.partial(_moe_visit_kernel, tm=tm, final=final),
        grid_spec=pltpu.PrefetchScalarGridSpec(
            num_scalar_prefetch=len(visits),
            grid=(n_visits,),
            in_specs=[by_tile(ROUTED_WIDTH), _resident(p['norm3_g'].shape),
                      expert_lo(up), expert_hi(up), expert_lo(up), expert_hi(up),
                      expert_lo(down), expert_hi(down), _resident(final_g.shape)],
            out_specs=by_tile(D_MODEL)),
        out_shape=jax.ShapeDtypeStruct((n, D_MODEL), F32),
        compiler_params=pltpu.CompilerParams(
            dimension_semantics=("arbitrary",), vmem_limit_bytes=VMEM_LIMIT_BYTES),
        name="moe_visits",
    )(*visits, xr_sorted, p['norm3_g'], p['w1'], p['w1'], p['w3'], p['w3'], p['w2'], p['w2'],
      final_g)


def _routed_moe(xr, p, final_g, *, rank_tile, tm, final):
    n, L, _ = xr.shape
    flat = xr.reshape(n * L, ROUTED_WIDTH)
    cls = flat[:, D_MODEL + INFO_CLASS].astype(jnp.int32)
    rank = flat[:, D_MODEL + INFO_RANK].astype(jnp.int32)
    pos, visits = _routing_plan(cls, rank, rank_tile=rank_tile, tm=tm)
    xr_sorted = _permute_rows(flat, pos, rows=tm, gather=False)
    y_sorted = _moe_visits(xr_sorted, visits, p, final_g, tm=tm, final=final)
    return _permute_rows(y_sorted, pos, rows=tm, gather=True).reshape(n, L, D_MODEL)


def _split_bf16(w):
    hi = w.astype(BF16)
    lo = (w - hi.astype(F32)).astype(BF16)
    return jnp.concatenate([hi, lo], axis=1)


def _layer_params(l, norm1_g, w_in, conv_a_w, conv_a_b, w_a_out, ln_v_g, ln_v_b, w_s, b_s,
                  w_b_out, conv_c_w, conv_c_b, ln_c_g, ln_c_b, w_c_out, w_o, norm2_g,
                  mem_norm_g, w_q, w_k, w_v, w_mo, norm3_g, rg_w, rg_b, re_w, re_b, w1, w3, w2):
    row = lambda a: a[l].reshape(1, -1)
    bf = lambda a: a[l].astype(BF16)
    w_router = jnp.concatenate([rg_w[l], re_w[l].reshape(D_MODEL, N_EXPERTS)], axis=1)
    b_router = jnp.concatenate([rg_b[l], re_b[l].reshape(N_EXPERTS)])
    pad = ROUTER_COLS - MOE_GROUPS - N_EXPERTS
    return dict(
        norm1_g=row(norm1_g), w_in=bf(w_in), conv_a_w=conv_a_w[l], conv_a_b=row(conv_a_b),
        w_a_out=bf(w_a_out), ln_v_g=row(ln_v_g), ln_v_b=row(ln_v_b), w_s=w_s[l], b_s=b_s[l],
        w_b_out=bf(w_b_out), conv_c_w=conv_c_w[l], conv_c_b=row(conv_c_b), ln_c_g=row(ln_c_g),
        ln_c_b=row(ln_c_b), w_c_out=bf(w_c_out), w_o=bf(w_o), norm2_g=row(norm2_g),
        mem_norm_g=row(mem_norm_g), w_q=bf(w_q), w_k=bf(w_k), w_v=bf(w_v), w_mo=bf(w_mo),
        norm3_g=row(norm3_g), w_router=_split_bf16(jnp.pad(w_router, ((0, 0), (0, pad)))),
        b_router=jnp.pad(b_router, (0, pad)).reshape(1, -1),
        w1=bf(w1), w3=bf(w3), w2=bf(w2))


def kernel(x_prompt, x_sample, state_conv_a, state_conv_c, cache_mem_k, cache_mem_v, mem_prompt, norm1_g, w_in, conv_a_w, conv_a_b, w_a_out, ln_v_g, ln_v_b, w_s, b_s, w_b_out, conv_c_w, conv_c_b, ln_c_g, ln_c_b, w_c_out, w_o, norm2_g, mem_norm_g, w_q, w_k, w_v, w_mo, norm3_g, rg_w, rg_b, re_w, re_b, w1, w3, w2, final_norm_g):
    depth = w_in.shape[0]
    n_p, n_s = x_prompt.shape[0], x_sample.shape[0]
    final_g = final_norm_g.reshape(1, -1)
    x_p, x_s = x_prompt, x_sample
    zero_a = jnp.zeros((n_p, CONV_A - 1, D_A), F32)
    zero_c = jnp.zeros((n_p, CONV_C - 1, D_C), F32)
    sa_p, sa_s, sc_p, sc_s, v_s, mk_p, mv_p = [], [], [], [], [], [], []
    for l in range(depth):
        p = _layer_params(l, norm1_g, w_in, conv_a_w, conv_a_b, w_a_out, ln_v_g, ln_v_b, w_s,
                          b_s, w_b_out, conv_c_w, conv_c_b, ln_c_g, ln_c_b, w_c_out, w_o,
                          norm2_g, mem_norm_g, w_q, w_k, w_v, w_mo, norm3_g, rg_w, rg_b, re_w,
                          re_b, w1, w3, w2)
        final = l == depth - 1
        mk, mv = _memory_kv(mem_prompt, p, tm=512)
        x_p, na, nc = _token_mix(x_p, zero_a, zero_c, p, tl=512, emit_v=False)
        xr_p = _cross_attend(x_p, mk, mv, p, tl=512, route=True)
        x_p = _routed_moe(xr_p, p, final_g, rank_tile=512, tm=512, final=final)
        sa_p.append(na)
        sc_p.append(nc)
        mk_p.append(mk.reshape(n_p, N_MEM, MEM_HEADS, MEM_HEAD_DIM))
        mv_p.append(mv.reshape(n_p, N_MEM, MEM_HEADS, MEM_HEAD_DIM))
        ck = cache_mem_k[l].reshape(n_s, N_MEM, D_MODEL)
        cv = cache_mem_v[l].reshape(n_s, N_MEM, D_MODEL)
        x_s, na, nc, vr = _token_mix(x_s, state_conv_a[l], state_conv_c[l], p, tl=x_s.shape[1],
                                     emit_v=True)
        x_s = _cross_attend(x_s, ck, cv, p, tl=x_s.shape[1], route=False)
        x_s = _hier_moe(x_s, p, final_g, tm=x_s.shape[0] * x_s.shape[1], final=final)
        sa_s.append(na)
        sc_s.append(nc)
        v_s.append(vr)
    return (x_p, x_s, jnp.stack(sa_p), jnp.stack(sa_s), jnp.stack(sc_p), jnp.stack(sc_s),
            jnp.stack(v_s), jnp.stack(mk_p), jnp.stack(mv_p))
```

```python
import functools

import jax
import jax.numpy as jnp
from jax import lax
from jax.experimental import pallas as pl
from jax.experimental.pallas import tpu as pltpu

D_MODEL = 1024
D_A = D_MODEL // 2
CONV_A = 3
D_B = D_MODEL // 2
B_GROUPS = 4
B_GROUP_DIM = D_B // B_GROUPS
B_CHUNK = 128
D_C = D_MODEL // 2
CONV_C = 31
N_MEM = 256
MEM_HEADS = 4
MEM_HEAD_DIM = D_MODEL // MEM_HEADS
MOE_GROUPS = 4
EXPERTS_PER_GROUP = 4
N_EXPERTS = MOE_GROUPS * EXPERTS_PER_GROUP
D_EXPERT = D_MODEL // 4
EPS = 1e-6
COL_XA = 0
COL_BA = COL_XA + D_A
COL_CA = COL_BA + D_A
COL_UV = COL_CA + D_A
COL_GLU = COL_UV + 2 * D_B
COL_GATE = COL_GLU + 2 * D_C
D_IN = COL_GATE + 3 * D_MODEL

SUBLANES = 8
LANES = 128
VMEM_LIMIT_BYTES = 56 * 1024 * 1024
DMA_QUEUES = 2

PA_NEW = SUBLANES
PA_HIST = PA_NEW - (CONV_A - 1)
PC_NEW = 32
PC_HIST = PC_NEW - (CONV_C - 1)
CONV_ROWS = 32
ROUTER_COLS = LANES
COL_EXPERT0 = MOE_GROUPS
PAIRS_PER_GROUP = EXPERTS_PER_GROUP * (EXPERTS_PER_GROUP - 1) // 2
N_CLASSES = MOE_GROUPS * PAIRS_PER_GROUP
INFO_CLASS, INFO_RANK, INFO_W_LO, INFO_W_HI = 0, 1, 2, 3
ROUTED_WIDTH = D_MODEL + ROUTER_COLS

BF16 = jnp.bfloat16
F32 = jnp.float32


def _dot(a, b):
    return jnp.dot(a, b, preferred_element_type=F32)


def _rmsnorm(x, g):
    ms = jnp.mean(x * x, axis=-1, keepdims=True)
    return x * lax.rsqrt(ms + EPS) * g


def _layernorm(x, g, b):
    xc = x - jnp.mean(x, axis=-1, keepdims=True)
    var = jnp.mean(xc * xc, axis=-1, keepdims=True)
    return xc * lax.rsqrt(var + EPS) * g + b


def _resident(shape):
    nd = len(shape)
    return pl.BlockSpec(shape, lambda *_: (0,) * nd, pipeline_mode=pl.Buffered(1))


def _mix_kernel(x_ref, sa_ref, sc_ref, g1_ref, win_ref, caw_ref, cab_ref, wa_ref, lvg_ref,
                lvb_ref, ws_ref, bs_ref, wb_ref, ccw_ref, ccb_ref, lcg_ref, lcb_ref, wc_ref,
                wo_ref, xo_ref, na_ref, nc_ref, *rest, tl, cl, emit_v):
    v_ref = rest[0] if emit_v else None
    pa_buf, pc_buf, s_buf, act_buf = rest[-4:]
    t = pl.program_id(1)

    @pl.when(t == 0)
    def _():
        pa_buf[PA_HIST:PA_NEW, :] = sa_ref[0]
        pc_buf[PC_HIST:PC_NEW, :] = sc_ref[0]

    x = x_ref[0]
    hb = _rmsnorm(x, g1_ref[...]).astype(BF16)

    def gate(i):
        lo = COL_GATE + i * D_MODEL
        return jax.nn.sigmoid(_dot(hb, win_ref[:, lo:lo + D_MODEL]))

    zg = _dot(hb, win_ref[:, COL_GLU:COL_GATE])
    pc_buf[PC_NEW:PC_NEW + tl, :] = zg[:, :D_C] * jax.nn.sigmoid(zg[:, D_C:])
    for r in range(tl // CONV_ROWS):
        base = PC_HIST + r * CONV_ROWS
        acc = jnp.broadcast_to(ccb_ref[...], (CONV_ROWS, D_C))
        for off in range(SUBLANES):
            part = None
            for k in range(CONV_C):
                if (base + k) % SUBLANES == off:
                    term = ccw_ref[k:k + 1, :] * pc_buf[base + k:base + k + CONV_ROWS, :]
                    part = term if part is None else part + term
            acc = acc + part
        ln = _layernorm(acc, lcg_ref[...], lcb_ref[...])
        act_buf[r * CONV_ROWS:(r + 1) * CONV_ROWS, :] = (ln * jax.nn.sigmoid(ln)).astype(BF16)
    new_c = pc_buf[PC_HIST + tl:PC_NEW + tl, :]
    nc_ref[0] = new_c
    pc_buf[PC_HIST:PC_NEW, :] = new_c

    za = _dot(hb, win_ref[:, COL_XA:COL_UV])
    pa_buf[PA_NEW:PA_NEW + tl, :] = za[:, 2 * D_A:] * za[:, :D_A]
    conv_a = cab_ref[...]
    for k in range(CONV_A):
        conv_a = conv_a + caw_ref[k:k + 1, :] * pa_buf[PA_HIST + k:PA_HIST + k + tl, :]
    y_a = _dot((za[:, D_A:2 * D_A] * conv_a).astype(BF16), wa_ref[...])
    merged = gate(0) * y_a
    new_a = pa_buf[PA_HIST + tl:PA_NEW + tl, :]
    na_ref[0] = new_a
    pa_buf[PA_HIST:PA_NEW, :] = new_a

    zuv = _dot(hb, win_ref[:, COL_UV:COL_GLU])
    uv = 0.5 * zuv * (1.0 + lax.erf(zuv * (0.5 ** 0.5)))
    v = _layernorm(uv[:, D_B:], lvg_ref[...], lvb_ref[...])
    if emit_v:
        v_ref[0] = v
    vb = v.astype(BF16)
    row = lax.broadcasted_iota(jnp.int32, (cl, cl), 0)
    col = lax.broadcasted_iota(jnp.int32, (cl, cl), 1)
    for g in range(B_GROUPS):
        wg = jnp.where(row >= col, ws_ref[g], 0.0).astype(BF16)
        for c in range(tl // cl):
            blk = vb[c * cl:(c + 1) * cl, g * B_GROUP_DIM:(g + 1) * B_GROUP_DIM]
            s_buf[c * cl:(c + 1) * cl, g * B_GROUP_DIM:(g + 1) * B_GROUP_DIM] = (
                _dot(wg, blk) + bs_ref[:, g * B_GROUP_DIM:(g + 1) * B_GROUP_DIM])
    y_b = _dot((uv[:, :D_B] * s_buf[...]).astype(BF16), wb_ref[...])
    merged = merged + gate(1) * y_b

    y_c = _dot(act_buf[...], wc_ref[...])
    merged = merged + gate(2) * y_c

    xo_ref[0] = x + _dot(merged.astype(BF16), wo_ref[...])


def _token_mix(x, state_a, state_c, p, *, tl, emit_v):
    n, L, _ = x.shape
    cl = min(L, B_CHUNK)
    assert L % tl == 0 and tl % cl == 0 and tl % CONV_ROWS == 0 and tl >= CONV_C - 1
    ws = p['w_s'][:, :cl, :cl]
    bs = jnp.repeat(p['b_s'][:, :cl].T, B_GROUP_DIM, axis=1)
    weights = [p['norm1_g'], p['w_in'], p['conv_a_w'], p['conv_a_b'], p['w_a_out'],
               p['ln_v_g'], p['ln_v_b'], ws, bs, p['w_b_out'], p['conv_c_w'], p['conv_c_b'],
               p['ln_c_g'], p['ln_c_b'], p['w_c_out'], p['w_o']]
    tile = lambda width: pl.BlockSpec((1, tl, width), lambda b, t: (b, t, 0))
    per_seq = lambda rows, width: pl.BlockSpec((1, rows, width), lambda b, t: (b, 0, 0))
    return pl.pallas_call(
        functools.partial(_mix_kernel, tl=tl, cl=cl, emit_v=emit_v),
        grid=(n, L // tl),
        in_specs=[tile(D_MODEL), per_seq(CONV_A - 1, D_A), per_seq(CONV_C - 1, D_C)]
        + [_resident(w.shape) for w in weights],
        out_specs=[tile(D_MODEL), per_seq(CONV_A - 1, D_A), per_seq(CONV_C - 1, D_C)]
        + ([tile(D_B)] if emit_v else []),
        out_shape=[jax.ShapeDtypeStruct((n, L, D_MODEL), F32),
                   jax.ShapeDtypeStruct((n, CONV_A - 1, D_A), F32),
                   jax.ShapeDtypeStruct((n, CONV_C - 1, D_C), F32)]
        + ([jax.ShapeDtypeStruct((n, L, D_B), F32)] if emit_v else []),
        scratch_shapes=[pltpu.VMEM((PA_NEW + tl, D_A), F32),
                        pltpu.VMEM((PC_NEW + tl, D_C), F32),
                        pltpu.VMEM((tl, D_B), F32),
                        pltpu.VMEM((tl, D_C), BF16)],
        compiler_params=pltpu.CompilerParams(
            dimension_semantics=("parallel", "arbitrary"), vmem_limit_bytes=VMEM_LIMIT_BYTES),
        name="token_mix",
    )(x, state_a, state_c, *weights)


def _memkv_kernel(m_ref, g_ref, wk_ref, wv_ref, k_ref, v_ref):
    mb = _rmsnorm(m_ref[...], g_ref[...]).astype(BF16)
    k_ref[...] = _dot(mb, wk_ref[...])
    v_ref[...] = _dot(mb, wv_ref[...])


def _memory_kv(mem, p, *, tm):
    n, m, _ = mem.shape
    rows = n * m
    assert rows % tm == 0
    tile = pl.BlockSpec((tm, D_MODEL), lambda i: (i, 0))
    weights = [p['mem_norm_g'], p['w_k'], p['w_v']]
    k, v = pl.pallas_call(
        _memkv_kernel,
        grid=(rows // tm,),
        in_specs=[tile] + [_resident(w.shape) for w in weights],
        out_specs=[tile, tile],
        out_shape=[jax.ShapeDtypeStruct((rows, D_MODEL), F32)] * 2,
        compiler_params=pltpu.CompilerParams(
            dimension_semantics=("parallel",), vmem_limit_bytes=VMEM_LIMIT_BYTES),
        name="memory_kv",
    )(mem.reshape(rows, D_MODEL), *weights)
    return k.reshape(n, m, D_MODEL), v.reshape(n, m, D_MODEL)


def _attn_kernel(x_ref, k_ref, v_ref, g_ref, wq_ref, wmo_ref, *rest, route):
    xo_ref, o_buf = rest[-2:]
    x = x_ref[0]
    hb = _rmsnorm(x, g_ref[...]).astype(BF16)
    qb = (_dot(hb, wq_ref[...]) * (MEM_HEAD_DIM ** -0.5)).astype(BF16)
    for h in range(MEM_HEADS):
        cols = slice(h * MEM_HEAD_DIM, (h + 1) * MEM_HEAD_DIM)
        kh = k_ref[0, :, cols].astype(BF16)
        vh = v_ref[0, :, cols].astype(BF16)
        s = lax.dot_general(qb[:, cols], kh, (((1,), (1,)), ((), ())),
                            preferred_element_type=F32)
        e = jnp.exp(s - jnp.max(s, axis=-1, keepdims=True))
        prob = e * (1.0 / jnp.sum(e, axis=-1, keepdims=True))
        o_buf[:, cols] = _dot(prob.astype(BF16), vh).astype(BF16)
    xn = x + _dot(o_buf[...], wmo_ref[...])
    if route:
        g3_ref, wr_ref, br_ref = rest[:3]
        xo_ref[0, :, :D_MODEL] = xn
        xo_ref[0, :, D_MODEL:] = _route_info(_router_logits(xn, g3_ref, wr_ref, br_ref))
    else:
        xo_ref[0] = xn


def _cross_attend(x, mem_k, mem_v, p, *, tl, route):
    n, L, _ = x.shape
    assert L % tl == 0
    width = D_MODEL + ROUTER_COLS if route else D_MODEL
    tile = pl.BlockSpec((1, tl, D_MODEL), lambda b, t: (b, t, 0))
    per_seq = pl.BlockSpec((1, N_MEM, D_MODEL), lambda b, t: (b, 0, 0))
    weights = [p['norm2_g'], p['w_q'], p['w_mo']]
    if route:
        weights += [p['norm3_g'], p['w_router'], p['b_router']]
    return pl.pallas_call(
        functools.partial(_attn_kernel, route=route),
        grid=(n, L // tl),
        in_specs=[tile, per_seq, per_seq] + [_resident(w.shape) for w in weights],
        out_specs=pl.BlockSpec((1, tl, width), lambda b, t: (b, t, 0)),
        out_shape=jax.ShapeDtypeStruct((n, L, width), F32),
        scratch_shapes=[pltpu.VMEM((tl, D_MODEL), BF16)],
        compiler_params=pltpu.CompilerParams(
            dimension_semantics=("parallel", "arbitrary"), vmem_limit_bytes=VMEM_LIMIT_BYTES),
        name="cross_attend",
    )(x, mem_k, mem_v, *weights)


def _router_logits(x, g_ref, wr_ref, br_ref):
    h = _rmsnorm(x, g_ref[...])
    h_hi = h.astype(BF16)
    h_lo = (h - h_hi.astype(F32)).astype(BF16)
    by_hi = _dot(h_hi, wr_ref[...])
    by_lo = _dot(h_lo, wr_ref[:, :ROUTER_COLS])
    return by_hi[:, :ROUTER_COLS] + (by_hi[:, ROUTER_COLS:] + by_lo) + br_ref[...]


def _route(logits):
    col = lax.broadcasted_iota(jnp.int32, logits.shape, 1)
    neg = -jnp.inf
    big = jnp.int32(ROUTER_COLS)
    lg = jnp.where(col < MOE_GROUPS, logits, neg)
    mg = jnp.max(lg, axis=-1, keepdims=True)
    p_top = 1.0 / jnp.sum(jnp.exp(lg - mg), axis=-1, keepdims=True)
    g_idx = jnp.min(jnp.where(lg == mg, col, big), axis=-1, keepdims=True)
    ecol = col - COL_EXPERT0
    in_group = (ecol >= 0) & (ecol < N_EXPERTS) & ((ecol >> 2) == g_idx)
    le = jnp.where(in_group, logits, neg)
    m1 = jnp.max(le, axis=-1, keepdims=True)
    i1 = jnp.min(jnp.where(le == m1, col, big), axis=-1, keepdims=True)
    le2 = jnp.where(col == i1, neg, le)
    m2 = jnp.max(le2, axis=-1, keepdims=True)
    i2 = jnp.min(jnp.where(le2 == m2, col, big), axis=-1, keepdims=True)
    e2 = jnp.exp(m2 - m1)
    w1 = p_top / (1.0 + e2)
    w2 = p_top * e2 / (1.0 + e2)
    return col, g_idx, i1, i2, w1, w2


def _expert_gates(logits):
    col, _, i1, i2, w1, w2 = _route(logits)
    return jnp.where(col == i1, w1, 0.0) + jnp.where(col == i2, w2, 0.0)


def _route_info(logits):
    tm = logits.shape[0]
    col, g_idx, i1, i2, w1, w2 = _route(logits)
    first_is_lo = i1 < i2
    a = (jnp.minimum(i1, i2) - COL_EXPERT0) & (EXPERTS_PER_GROUP - 1)
    b = (jnp.maximum(i1, i2) - COL_EXPERT0) & (EXPERTS_PER_GROUP - 1)
    cls = g_idx * PAIRS_PER_GROUP + ((a * (2 * EXPERTS_PER_GROUP - 1 - a)) >> 1) + (b - a - 1)
    onehot = jnp.where(col == cls, 1.0, 0.0)
    row_i = lax.broadcasted_iota(jnp.int32, (tm, tm), 0)
    col_i = lax.broadcasted_iota(jnp.int32, (tm, tm), 1)
    tri = jnp.where(row_i >= col_i, 1.0, 0.0).astype(BF16)
    running = _dot(tri, onehot.astype(BF16))
    rank = jnp.sum(running * onehot, axis=-1, keepdims=True) - 1.0
    w_lo = jnp.where(first_is_lo, w1, w2)
    w_hi = jnp.where(first_is_lo, w2, w1)
    return jnp.where(col == INFO_CLASS, cls.astype(F32),
                     jnp.where(col == INFO_RANK, rank,
                               jnp.where(col == INFO_W_LO, w_lo,
                                         jnp.where(col == INFO_W_HI, w_hi, 0.0))))


def _moe_kernel(x_ref, g_ref, wr_ref, br_ref, w1_ref, w3_ref, w2_ref, gf_ref, o_ref, *, final):
    x = x_ref[...]
    gates = _expert_gates(_router_logits(x, g_ref, wr_ref, br_ref))
    hb = _rmsnorm(x, g_ref[...]).astype(BF16)
    acc = x
    for e in range(N_EXPERTS):
        a = _dot(hb, w1_ref[e])
        hid = a * jax.nn.sigmoid(a) * _dot(hb, w3_ref[e])
        g = gates[:, COL_EXPERT0 + e:COL_EXPERT0 + e + 1]
        acc = acc + _dot((hid * g).astype(BF16), w2_ref[e])
    o_ref[...] = _rmsnorm(acc, gf_ref[...]) if final else acc


def _hier_moe(x, p, final_g, *, tm, final):
    shape = x.shape
    t = x.reshape(-1, D_MODEL)
    rows = t.shape[0]
    assert rows % tm == 0
    tile = pl.BlockSpec((tm, D_MODEL), lambda i: (i, 0))
    weights = [p['norm3_g'], p['w_router'], p['b_router'], p['w1'], p['w3'], p['w2'], final_g]
    out = pl.pallas_call(
        functools.partial(_moe_kernel, final=final),
        grid=(rows // tm,),
        in_specs=[tile] + [_resident(w.shape) for w in weights],
        out_specs=tile,
        out_shape=jax.ShapeDtypeStruct((rows, D_MODEL), F32),
        compiler_params=pltpu.CompilerParams(
            dimension_semantics=("parallel",), vmem_limit_bytes=VMEM_LIMIT_BYTES),
        name="hier_moe",
    )(t, *weights)
    return out.reshape(shape)


def _permute_kernel(pos_ref, src_ref, dst_ref, sem, *, rows, gather):
    def row_copy(j):
        p = pos_ref[0, 0, j]
        if gather:
            return pltpu.make_async_copy(src_ref.at[pl.ds(p, 1)], dst_ref.at[pl.ds(j, 1)], sem)
        return pltpu.make_async_copy(src_ref.at[pl.ds(j, 1)], dst_ref.at[pl.ds(p, 1)], sem)

    for j in range(rows):
        row_copy(j).start(priority=j % DMA_QUEUES)

    def wait(j, carry):
        row_copy(0).wait()
        return carry

    lax.fori_loop(0, rows, wait, 0, unroll=8)


def _permute_rows(src, pos, *, rows, gather):
    n, width = src.shape
    assert n % rows == 0
    steps = n // rows
    tile = pl.BlockSpec((rows, width), lambda s: (s, 0))
    hbm = pl.BlockSpec(memory_space=pl.ANY)
    return pl.pallas_call(
        functools.partial(_permute_kernel, rows=rows, gather=gather),
        grid=(steps,),
        in_specs=[pl.BlockSpec((1, 1, rows), lambda s: (s, 0, 0), memory_space=pltpu.SMEM),
                  hbm if gather else tile],
        out_specs=tile if gather else hbm,
        out_shape=jax.ShapeDtypeStruct((n, width), src.dtype),
        scratch_shapes=[pltpu.SemaphoreType.DMA(())],
        compiler_params=pltpu.CompilerParams(dimension_semantics=("arbitrary",)),
        name="gather_rows" if gather else "scatter_rows",
    )(pos.reshape(steps, 1, rows), src)


def _routing_plan(cls, rank, *, rank_tile, tm):
    n = cls.shape[0]
    tiles = n // tm
    rank_tiles = n // rank_tile
    onehot = cls.reshape(rank_tiles, rank_tile, 1) == jnp.arange(N_CLASSES, dtype=jnp.int32)
    counts = jnp.sum(onehot.astype(jnp.int32), axis=1)
    class_end = jnp.cumsum(jnp.sum(counts, axis=0))
    class_start = class_end - jnp.sum(counts, axis=0)
    tile_base = class_start[None, :] + jnp.cumsum(counts, axis=0) - counts
    pos = jnp.sum(jnp.where(onehot, tile_base[:, None, :], 0), axis=2).reshape(n) + rank
    tile_cut = jnp.arange(tiles, dtype=jnp.int32) * tm
    class_cut = class_start[1:]
    tile_slot = jnp.arange(tiles) + jnp.sum(class_cut[None, :] < tile_cut[:, None], axis=1)
    class_slot = (jnp.arange(N_CLASSES - 1)
                  + jnp.sum(tile_cut[None, :] <= class_cut[:, None], axis=1))
    slots = jnp.arange(tiles + N_CLASSES - 1)[:, None]
    lo = (jnp.sum(jnp.where(tile_slot[None, :] == slots, tile_cut[None, :], 0), axis=1)
          + jnp.sum(jnp.where(class_slot[None, :] == slots, class_cut[None, :], 0), axis=1))
    hi = jnp.concatenate([lo[1:], jnp.array([n], lo.dtype)])
    tile = jnp.minimum(lo // tm, tiles - 1)
    c = jnp.minimum(jnp.sum(class_end[None, :] <= lo[:, None], axis=1), N_CLASSES - 1)
    first = ((hi > lo) & (lo == tile * tm)).astype(jnp.int32)
    group, pair = c // PAIRS_PER_GROUP, c % PAIRS_PER_GROUP
    a = (pair >= EXPERTS_PER_GROUP - 1).astype(jnp.int32) + (pair >= 2 * EXPERTS_PER_GROUP - 3)
    b = pair - ((a * (2 * EXPERTS_PER_GROUP - 1 - a)) >> 1) + a + 1
    e_lo = group * EXPERTS_PER_GROUP + a
    e_hi = group * EXPERTS_PER_GROUP + b
    as_i32 = lambda a: a.astype(jnp.int32)
    return as_i32(pos), tuple(as_i32(a) for a in (tile, e_lo, e_hi, lo, hi, first))


def _moe_visit_kernel(tile_ref, elo_ref, ehi_ref, lo_ref, hi_ref, first_ref, xr_ref, g_ref,
                      w1a_ref, w1b_ref, w3a_ref, w3b_ref, w2a_ref, w2b_ref, gf_ref, o_ref, *,
                      tm, final):
    v = pl.program_id(0)
    lo, hi = lo_ref[v], hi_ref[v]

    @pl.when(hi > lo)
    def _():
        x = xr_ref[:, :D_MODEL]
        info = xr_ref[:, D_MODEL:]
        hb = _rmsnorm(x, g_ref[...]).astype(BF16)

        def hidden(w1_ref, w3_ref, gate):
            a = _dot(hb, w1_ref[0])
            return (a * jax.nn.sigmoid(a) * _dot(hb, w3_ref[0]) * gate).astype(BF16)

        h_lo = hidden(w1a_ref, w3a_ref, info[:, INFO_W_LO:INFO_W_LO + 1])
        h_hi = hidden(w1b_ref, w3b_ref, info[:, INFO_W_HI:INFO_W_HI + 1])
        res = x + (_dot(h_lo, w2a_ref[0]) + _dot(h_hi, w2b_ref[0]))
        if final:
            res = _rmsnorm(res, gf_ref[...])
        row = tile_ref[v] * tm + lax.broadcasted_iota(jnp.int32, (tm, 1), 0)
        mine = (row >= lo) & (row < hi)

        @pl.when(first_ref[v] == 1)
        def _():
            o_ref[...] = jnp.where(mine, res, 0.0)

        @pl.when(first_ref[v] == 0)
        def _():
            o_ref[...] = jnp.where(mine, res, o_ref[...])


def _moe_visits(xr_sorted, visits, p, final_g, *, tm, final):
    n = xr_sorted.shape[0]
    n_visits = visits[0].shape[0]
    by_tile = lambda width: pl.BlockSpec((tm, width), lambda v, tile, *_: (tile[v], 0))
    expert_lo = lambda shape: pl.BlockSpec((1,) + shape, lambda v, tile, elo, *_: (elo[v], 0, 0))
    expert_hi = lambda shape: pl.BlockSpec((1,) + shape,
                                           lambda v, tile, elo, ehi, *_: (ehi[v], 0, 0))
    up, down = (D_MODEL, D_EXPERT), (D_EXPERT, D_MODEL)
    return pl.pallas_call(
        functools.partial(_moe_visit_kernel, tm=tm, final=final),
        grid_spec=pltpu.PrefetchScalarGridSpec(
            num_scalar_prefetch=len(visits),
            grid=(n_visits,),
            in_specs=[by_tile(ROUTED_WIDTH), _resident(p['norm3_g'].shape),
                      expert_lo(up), expert_hi(up), expert_lo(up), expert_hi(up),
                      expert_lo(down), expert_hi(down), _resident(final_g.shape)],
            out_specs=by_tile(D_MODEL)),
        out_shape=jax.ShapeDtypeStruct((n, D_MODEL), F32),
        compiler_params=pltpu.CompilerParams(
            dimension_semantics=("arbitrary",), vmem_limit_bytes=VMEM_LIMIT_BYTES),
        name="moe_visits",
    )(*visits, xr_sorted, p['norm3_g'], p['w1'], p['w1'], p['w3'], p['w3'], p['w2'], p['w2'],
      final_g)


def _routed_moe(xr, p, final_g, *, rank_tile, tm, final):
    n, L, _ = xr.shape
    flat = xr.reshape(n * L, ROUTED_WIDTH)
    cls = flat[:, D_MODEL + INFO_CLASS].astype(jnp.int32)
    rank = flat[:, D_MODEL + INFO_RANK].astype(jnp.int32)
    pos, visits = _routing_plan(cls, rank, rank_tile=rank_tile, tm=tm)
    xr_sorted = _permute_rows(flat, pos, rows=tm, gather=False)
    y_sorted = _moe_visits(xr_sorted, visits, p, final_g, tm=tm, final=final)
    return _permute_rows(y_sorted, pos, rows=tm, gather=True).reshape(n, L, D_MODEL)


def _split_bf16(w):
    hi = w.astype(BF16)
    lo = (w - hi.astype(F32)).astype(BF16)
    return jnp.concatenate([hi, lo], axis=1)


def _layer_params(l, norm1_g, w_in, conv_a_w, conv_a_b, w_a_out, ln_v_g, ln_v_b, w_s, b_s,
                  w_b_out, conv_c_w, conv_c_b, ln_c_g, ln_c_b, w_c_out, w_o, norm2_g,
                  mem_norm_g, w_q, w_k, w_v, w_mo, norm3_g, rg_w, rg_b, re_w, re_b, w1, w3, w2):
    row = lambda a: a[l].reshape(1, -1)
    bf = lambda a: a[l].astype(BF16)
    w_router = jnp.concatenate([rg_w[l], re_w[l].reshape(D_MODEL, N_EXPERTS)], axis=1)
    b_router = jnp.concatenate([rg_b[l], re_b[l].reshape(N_EXPERTS)])
    pad = ROUTER_COLS - MOE_GROUPS - N_EXPERTS
    return dict(
        norm1_g=row(norm1_g), w_in=bf(w_in), conv_a_w=conv_a_w[l], conv_a_b=row(conv_a_b),
        w_a_out=bf(w_a_out), ln_v_g=row(ln_v_g), ln_v_b=row(ln_v_b), w_s=w_s[l], b_s=b_s[l],
        w_b_out=bf(w_b_out), conv_c_w=conv_c_w[l], conv_c_b=row(conv_c_b), ln_c_g=row(ln_c_g),
        ln_c_b=row(ln_c_b), w_c_out=bf(w_c_out), w_o=bf(w_o), norm2_g=row(norm2_g),
        mem_norm_g=row(mem_norm_g), w_q=bf(w_q), w_k=bf(w_k), w_v=bf(w_v), w_mo=bf(w_mo),
        norm3_g=row(norm3_g), w_router=_split_bf16(jnp.pad(w_router, ((0, 0), (0, pad)))),
        b_router=jnp.pad(b_router, (0, pad)).reshape(1, -1),
        w1=bf(w1), w3=bf(w3), w2=bf(w2))


def kernel(x_prompt, x_sample, state_conv_a, state_conv_c, cache_mem_k, cache_mem_v, mem_prompt, norm1_g, w_in, conv_a_w, conv_a_b, w_a_out, ln_v_g, ln_v_b, w_s, b_s, w_b_out, conv_c_w, conv_c_b, ln_c_g, ln_c_b, w_c_out, w_o, norm2_g, mem_norm_g, w_q, w_k, w_v, w_mo, norm3_g, rg_w, rg_b, re_w, re_b, w1, w3, w2, final_norm_g):
    depth = w_in.shape[0]
    n_p, n_s = x_prompt.shape[0], x_sample.shape[0]
    final_g = final_norm_g.reshape(1, -1)
    x_p, x_s = x_prompt, x_sample
    zero_a = jnp.zeros((n_p, CONV_A - 1, D_A), F32)
    zero_c = jnp.zeros((n_p, CONV_C - 1, D_C), F32)
    sa_p, sa_s, sc_p, sc_s, v_s, mk_p, mv_p = [], [], [], [], [], [], []
    for l in range(depth):
        p = _layer_params(l, norm1_g, w_in, conv_a_w, conv_a_b, w_a_out, ln_v_g, ln_v_b, w_s,
                          b_s, w_b_out, conv_c_w, conv_c_b, ln_c_g, ln_c_b, w_c_out, w_o,
                          norm2_g, mem_norm_g, w_q, w_k, w_v, w_mo, norm3_g, rg_w, rg_b, re_w,
                          re_b, w1, w3, w2)
        final = l == depth - 1
        mk, mv = _memory_kv(mem_prompt, p, tm=512)
        x_p, na, nc = _token_mix(x_p, zero_a, zero_c, p, tl=512, emit_v=False)
        xr_p = _cross_attend(x_p, mk, mv, p, tl=512, route=True)
        x_p = _routed_moe(xr_p, p, final_g, rank_tile=512, tm=512, final=final)
        sa_p.append(na)
        sc_p.append(nc)
        mk_p.append(mk.reshape(n_p, N_MEM, MEM_HEADS, MEM_HEAD_DIM))
        mv_p.append(mv.reshape(n_p, N_MEM, MEM_HEADS, MEM_HEAD_DIM))
        ck = cache_mem_k[l].reshape(n_s, N_MEM, D_MODEL)
        cv = cache_mem_v[l].reshape(n_s, N_MEM, D_MODEL)
        x_s, na, nc, vr = _token_mix(x_s, state_conv_a[l], state_conv_c[l], p, tl=x_s.shape[1],
                                     emit_v=True)
        x_s = _cross_attend(x_s, ck, cv, p, tl=x_s.shape[1], route=False)
        x_s = _hier_moe(x_s, p, final_g, tm=x_s.shape[0] * x_s.shape[1], final=final)
        sa_s.append(na)
        sc_s.append(nc)
        v_s.append(vr)
    return (x_p, x_s, jnp.stack(sa_p), jnp.stack(sa_s), jnp.stack(sc_p), jnp.stack(sc_s),
            jnp.stack(v_s), jnp.stack(mk_p), jnp.stack(mv_p))
```

```python
import functools

import jax
import jax.numpy as jnp
from jax import lax
from jax.experimental import pallas as pl
from jax.experimental.pallas import tpu as pltpu

D_MODEL = 1024
D_A = D_MODEL // 2
CONV_A = 3
D_B = D_MODEL // 2
B_GROUPS = 4
B_GROUP_DIM = D_B // B_GROUPS
B_CHUNK = 128
D_C = D_MODEL // 2
CONV_C = 31
N_MEM = 256
MEM_HEADS = 4
MEM_HEAD_DIM = D_MODEL // MEM_HEADS
MOE_GROUPS = 4
EXPERTS_PER_GROUP = 4
N_EXPERTS = MOE_GROUPS * EXPERTS_PER_GROUP
D_EXPERT = D_MODEL // 4
EPS = 1e-6
COL_XA = 0
COL_BA = COL_XA + D_A
COL_CA = COL_BA + D_A
COL_UV = COL_CA + D_A
COL_GLU = COL_UV + 2 * D_B
COL_GATE = COL_GLU + 2 * D_C
D_IN = COL_GATE + 3 * D_MODEL

SUBLANES = 8
LANES = 128
VMEM_LIMIT_BYTES = 56 * 1024 * 1024
DMA_QUEUES = 2

PA_NEW = SUBLANES
PA_HIST = PA_NEW - (CONV_A - 1)
PC_NEW = 32
PC_HIST = PC_NEW - (CONV_C - 1)
CONV_ROWS = 32
ROUTER_COLS = LANES
COL_EXPERT0 = MOE_GROUPS
PAIRS_PER_GROUP = EXPERTS_PER_GROUP * (EXPERTS_PER_GROUP - 1) // 2
N_CLASSES = MOE_GROUPS * PAIRS_PER_GROUP
INFO_CLASS, INFO_RANK, INFO_W_LO, INFO_W_HI = 0, 1, 2, 3
ROUTED_WIDTH = D_MODEL + ROUTER_COLS
PERMUTE_ROWS = 1024

BF16 = jnp.bfloat16
F32 = jnp.float32


def _dot(a, b):
    return jnp.dot(a, b, preferred_element_type=F32)


def _rmsnorm(x, g):
    ms = jnp.mean(x * x, axis=-1, keepdims=True)
    return x * lax.rsqrt(ms + EPS) * g


def _layernorm(x, g, b):
    xc = x - jnp.mean(x, axis=-1, keepdims=True)
    var = jnp.mean(xc * xc, axis=-1, keepdims=True)
    return xc * lax.rsqrt(var + EPS) * g + b


def _resident(shape):
    nd = len(shape)
    return pl.BlockSpec(shape, lambda *_: (0,) * nd, pipeline_mode=pl.Buffered(1))


def _mix_kernel(x_ref, sa_ref, sc_ref, g1_ref, win_ref, caw_ref, cab_ref, wa_ref, lvg_ref,
                lvb_ref, ws_ref, bs_ref, wb_ref, ccw_ref, ccb_ref, lcg_ref, lcb_ref, wc_ref,
                wo_ref, xo_ref, na_ref, nc_ref, *rest, tl, cl, emit_v):
    v_ref = rest[0] if emit_v else None
    pa_buf, pc_buf, s_buf, act_buf = rest[-4:]
    t = pl.program_id(1)

    @pl.when(t == 0)
    def _():
        pa_buf[PA_HIST:PA_NEW, :] = sa_ref[0]
        pc_buf[PC_HIST:PC_NEW, :] = sc_ref[0]

    x = x_ref[0]
    hb = _rmsnorm(x, g1_ref[...]).astype(BF16)

    def gate(i):
        lo = COL_GATE + i * D_MODEL
        return jax.nn.sigmoid(_dot(hb, win_ref[:, lo:lo + D_MODEL]))

    zg = _dot(hb, win_ref[:, COL_GLU:COL_GATE])
    pc_buf[PC_NEW:PC_NEW + tl, :] = zg[:, :D_C] * jax.nn.sigmoid(zg[:, D_C:])
    for r in range(tl // CONV_ROWS):
        base = PC_HIST + r * CONV_ROWS
        acc = jnp.broadcast_to(ccb_ref[...], (CONV_ROWS, D_C))
        for off in range(SUBLANES):
            part = None
            for k in range(CONV_C):
                if (base + k) % SUBLANES == off:
                    term = ccw_ref[k:k + 1, :] * pc_buf[base + k:base + k + CONV_ROWS, :]
                    part = term if part is None else part + term
            acc = acc + part
        ln = _layernorm(acc, lcg_ref[...], lcb_ref[...])
        act_buf[r * CONV_ROWS:(r + 1) * CONV_ROWS, :] = (ln * jax.nn.sigmoid(ln)).astype(BF16)
    new_c = pc_buf[PC_HIST + tl:PC_NEW + tl, :]
    nc_ref[0] = new_c
    pc_buf[PC_HIST:PC_NEW, :] = new_c

    za = _dot(hb, win_ref[:, COL_XA:COL_UV])
    pa_buf[PA_NEW:PA_NEW + tl, :] = za[:, 2 * D_A:] * za[:, :D_A]
    conv_a = cab_ref[...]
    for k in range(CONV_A):
        conv_a = conv_a + caw_ref[k:k + 1, :] * pa_buf[PA_HIST + k:PA_HIST + k + tl, :]
    y_a = _dot((za[:, D_A:2 * D_A] * conv_a).astype(BF16), wa_ref[...])
    merged = gate(0) * y_a
    new_a = pa_buf[PA_HIST + tl:PA_NEW + tl, :]
    na_ref[0] = new_a
    pa_buf[PA_HIST:PA_NEW, :] = new_a

    zuv = _dot(hb, win_ref[:, COL_UV:COL_GLU])
    uv = 0.5 * zuv * (1.0 + lax.erf(zuv * (0.5 ** 0.5)))
    v = _layernorm(uv[:, D_B:], lvg_ref[...], lvb_ref[...])
    if emit_v:
        v_ref[0] = v
    vb = v.astype(BF16)
    row = lax.broadcasted_iota(jnp.int32, (cl, cl), 0)
    col = lax.broadcasted_iota(jnp.int32, (cl, cl), 1)
    for g in range(B_GROUPS):
        wg = jnp.where(row >= col, ws_ref[g], 0.0).astype(BF16)
        for c in range(tl // cl):
            blk = vb[c * cl:(c + 1) * cl, g * B_GROUP_DIM:(g + 1) * B_GROUP_DIM]
            s_buf[c * cl:(c + 1) * cl, g * B_GROUP_DIM:(g + 1) * B_GROUP_DIM] = (
                _dot(wg, blk) + bs_ref[:, g * B_GROUP_DIM:(g + 1) * B_GROUP_DIM])
    y_b = _dot((uv[:, :D_B] * s_buf[...]).astype(BF16), wb_ref[...])
    merged = merged + gate(1) * y_b

    y_c = _dot(act_buf[...], wc_ref[...])
    merged = merged + gate(2) * y_c

    xo_ref[0] = x + _dot(merged.astype(BF16), wo_ref[...])


def _token_mix(x, state_a, state_c, p, *, tl, emit_v):
    n, L, _ = x.shape
    cl = min(L, B_CHUNK)
    assert L % tl == 0 and tl % cl == 0 and tl % CONV_ROWS == 0 and tl >= CONV_C - 1
    ws = p['w_s'][:, :cl, :cl]
    bs = jnp.repeat(p['b_s'][:, :cl].T, B_GROUP_DIM, axis=1)
    weights = [p['norm1_g'], p['w_in'], p['conv_a_w'], p['conv_a_b'], p['w_a_out'],
               p['ln_v_g'], p['ln_v_b'], ws, bs, p['w_b_out'], p['conv_c_w'], p['conv_c_b'],
               p['ln_c_g'], p['ln_c_b'], p['w_c_out'], p['w_o']]
    tile = lambda width: pl.BlockSpec((1, tl, width), lambda b, t: (b, t, 0))
    per_seq = lambda rows, width: pl.BlockSpec((1, rows, width), lambda b, t: (b, 0, 0))
    return pl.pallas_call(
        functools.partial(_mix_kernel, tl=tl, cl=cl, emit_v=emit_v),
        grid=(n, L // tl),
        in_specs=[tile(D_MODEL), per_seq(CONV_A - 1, D_A), per_seq(CONV_C - 1, D_C)]
        + [_resident(w.shape) for w in weights],
        out_specs=[tile(D_MODEL), per_seq(CONV_A - 1, D_A), per_seq(CONV_C - 1, D_C)]
        + ([tile(D_B)] if emit_v else []),
        out_shape=[jax.ShapeDtypeStruct((n, L, D_MODEL), F32),
                   jax.ShapeDtypeStruct((n, CONV_A - 1, D_A), F32),
                   jax.ShapeDtypeStruct((n, CONV_C - 1, D_C), F32)]
        + ([jax.ShapeDtypeStruct((n, L, D_B), F32)] if emit_v else []),
        scratch_shapes=[pltpu.VMEM((PA_NEW + tl, D_A), F32),
                        pltpu.VMEM((PC_NEW + tl, D_C), F32),
                        pltpu.VMEM((tl, D_B), F32),
                        pltpu.VMEM((tl, D_C), BF16)],
        compiler_params=pltpu.CompilerParams(
            dimension_semantics=("parallel", "arbitrary"), vmem_limit_bytes=VMEM_LIMIT_BYTES),
        name="token_mix",
    )(x, state_a, state_c, *weights)


def _memkv_kernel(m_ref, g_ref, wk_ref, wv_ref, kf_ref, vf_ref, kh_ref, vh_ref):
    mb = _rmsnorm(m_ref[...], g_ref[0]).astype(BF16)
    for w_ref, flat_ref, heads_ref in ((wk_ref, kf_ref, kh_ref), (wv_ref, vf_ref, vh_ref)):
        proj = _dot(mb, w_ref[0])
        flat_ref[0] = proj
        for h in range(MEM_HEADS):
            heads_ref[0, :, h, :] = proj[:, h * MEM_HEAD_DIM:(h + 1) * MEM_HEAD_DIM]


def _memory_kv(mem, mem_norm_g, w_k, w_v, *, tm):
    depth = w_k.shape[0]
    n, m, _ = mem.shape
    rows = n * m
    assert rows % tm == 0
    flat = pl.BlockSpec((1, tm, D_MODEL), lambda l, i: (l, i, 0))
    heads = pl.BlockSpec((1, tm, MEM_HEADS, MEM_HEAD_DIM), lambda l, i: (l, i, 0, 0))
    weight = pl.BlockSpec((1, D_MODEL, D_MODEL), lambda l, i: (l, 0, 0))
    return pl.pallas_call(
        _memkv_kernel,
        grid=(depth, rows // tm),
        in_specs=[pl.BlockSpec((tm, D_MODEL), lambda l, i: (i, 0)),
                  pl.BlockSpec((1, 1, D_MODEL), lambda l, i: (l, 0, 0)), weight, weight],
        out_specs=[flat, flat, heads, heads],
        out_shape=[jax.ShapeDtypeStruct((depth, rows, D_MODEL), F32)] * 2
        + [jax.ShapeDtypeStruct((depth, rows, MEM_HEADS, MEM_HEAD_DIM), F32)] * 2,
        compiler_params=pltpu.CompilerParams(
            dimension_semantics=("parallel", "parallel"), vmem_limit_bytes=VMEM_LIMIT_BYTES),
        name="memory_kv",
    )(mem.reshape(rows, D_MODEL), mem_norm_g.reshape(depth, 1, D_MODEL), w_k.astype(BF16),
      w_v.astype(BF16))


def _attention(x, k_ref, v_ref, g_ref, wq_ref, wmo_ref, o_buf):
    hb = _rmsnorm(x, g_ref[...]).astype(BF16)
    qb = (_dot(hb, wq_ref[...]) * (MEM_HEAD_DIM ** -0.5)).astype(BF16)
    heads = [slice(h * MEM_HEAD_DIM, (h + 1) * MEM_HEAD_DIM) for h in range(MEM_HEADS)]
    scores = [lax.dot_general(qb[:, cols], k_ref[0, 0, :, cols].astype(BF16),
                              (((1,), (1,)), ((), ())), preferred_element_type=F32)
              for cols in heads]
    for s, cols in zip(scores, heads):
        e = jnp.exp(s - jnp.max(s, axis=-1, keepdims=True))
        prob = e * (1.0 / jnp.sum(e, axis=-1, keepdims=True))
        o_buf[:, cols] = _dot(prob.astype(BF16), v_ref[0, 0, :, cols].astype(BF16)).astype(BF16)
    return x + _dot(o_buf[...], wmo_ref[...])


def _attn_kernel(x_ref, k_ref, v_ref, g_ref, wq_ref, wmo_ref, *rest, route):
    xo_ref, o_buf = rest[-2:]
    xn = _attention(x_ref[0], k_ref, v_ref, g_ref, wq_ref, wmo_ref, o_buf)
    if route:
        g3_ref, wr_ref, br_ref = rest[:3]
        xo_ref[:, :D_MODEL] = xn
        xo_ref[:, D_MODEL:] = _route_info(_router_logits(xn, g3_ref, wr_ref, br_ref))
    else:
        xo_ref[0] = xn


def _cross_attend(x, mem_k, mem_v, layer, p, *, tl, route):
    n, L, _ = x.shape
    assert L % tl == 0
    per_seq_tiles = L // tl
    tile = pl.BlockSpec((1, tl, D_MODEL), lambda b, t: (b, t, 0))
    per_seq = pl.BlockSpec((1, 1, N_MEM, D_MODEL), lambda b, t: (layer, b, 0, 0))
    weights = [p['norm2_g'], p['w_q'], p['w_mo']]
    if route:
        weights += [p['norm3_g'], p['w_router'], p['b_router']]
        out_spec = pl.BlockSpec((tl, ROUTED_WIDTH), lambda b, t: (b * per_seq_tiles + t, 0))
        out_shape = jax.ShapeDtypeStruct((n * L, ROUTED_WIDTH), F32)
    else:
        out_spec, out_shape = tile, jax.ShapeDtypeStruct((n, L, D_MODEL), F32)
    return pl.pallas_call(
        functools.partial(_attn_kernel, route=route),
        grid=(n, per_seq_tiles),
        in_specs=[tile, per_seq, per_seq] + [_resident(w.shape) for w in weights],
        out_specs=out_spec,
        out_shape=out_shape,
        scratch_shapes=[pltpu.VMEM((tl, D_MODEL), BF16)],
        compiler_params=pltpu.CompilerParams(
            dimension_semantics=("parallel", "arbitrary"), vmem_limit_bytes=VMEM_LIMIT_BYTES),
        name="cross_attend",
    )(x, mem_k, mem_v, *weights)


def _router_logits(x, g_ref, wr_ref, br_ref):
    h = _rmsnorm(x, g_ref[...])
    h_hi = h.astype(BF16)
    h_lo = (h - h_hi.astype(F32)).astype(BF16)
    by_hi = _dot(h_hi, wr_ref[...])
    by_lo = _dot(h_lo, wr_ref[:, :ROUTER_COLS])
    return by_hi[:, :ROUTER_COLS] + (by_hi[:, ROUTER_COLS:] + by_lo) + br_ref[...]


def _route(logits):
    col = lax.broadcasted_iota(jnp.int32, logits.shape, 1)
    neg = -jnp.inf
    big = jnp.int32(ROUTER_COLS)
    lg = jnp.where(col < MOE_GROUPS, logits, neg)
    mg = jnp.max(lg, axis=-1, keepdims=True)
    p_top = 1.0 / jnp.sum(jnp.exp(lg - mg), axis=-1, keepdims=True)
    g_idx = jnp.min(jnp.where(lg == mg, col, big), axis=-1, keepdims=True)
    ecol = col - COL_EXPERT0
    in_group = (ecol >= 0) & (ecol < N_EXPERTS) & ((ecol >> 2) == g_idx)
    le = jnp.where(in_group, logits, neg)
    m1 = jnp.max(le, axis=-1, keepdims=True)
    i1 = jnp.min(jnp.where(le == m1, col, big), axis=-1, keepdims=True)
    le2 = jnp.where(col == i1, neg, le)
    m2 = jnp.max(le2, axis=-1, keepdims=True)
    i2 = jnp.min(jnp.where(le2 == m2, col, big), axis=-1, keepdims=True)
    e2 = jnp.exp(m2 - m1)
    w1 = p_top / (1.0 + e2)
    w2 = p_top * e2 / (1.0 + e2)
    return col, g_idx, i1, i2, w1, w2


def _expert_gates(logits):
    col, _, i1, i2, w1, w2 = _route(logits)
    return jnp.where(col == i1, w1, 0.0) + jnp.where(col == i2, w2, 0.0)


def _route_info(logits):
    tm = logits.shape[0]
    col, g_idx, i1, i2, w1, w2 = _route(logits)
    first_is_lo = i1 < i2
    a = (jnp.minimum(i1, i2) - COL_EXPERT0) & (EXPERTS_PER_GROUP - 1)
    b = (jnp.maximum(i1, i2) - COL_EXPERT0) & (EXPERTS_PER_GROUP - 1)
    cls = g_idx * PAIRS_PER_GROUP + ((a * (2 * EXPERTS_PER_GROUP - 1 - a)) >> 1) + (b - a - 1)
    onehot = jnp.where(col == cls, 1.0, 0.0)
    row_i = lax.broadcasted_iota(jnp.int32, (tm, tm), 0)
    col_i = lax.broadcasted_iota(jnp.int32, (tm, tm), 1)
    tri = jnp.where(row_i >= col_i, 1.0, 0.0).astype(BF16)
    running = _dot(tri, onehot.astype(BF16))
    rank = jnp.sum(running * onehot, axis=-1, keepdims=True) - 1.0
    w_lo = jnp.where(first_is_lo, w1, w2)
    w_hi = jnp.where(first_is_lo, w2, w1)
    return jnp.where(col == INFO_CLASS, cls.astype(F32),
                     jnp.where(col == INFO_RANK, rank,
                               jnp.where(col == INFO_W_LO, w_lo,
                                         jnp.where(col == INFO_W_HI, w_hi, 0.0))))


def _moe_kernel(x_ref, g_ref, wr_ref, br_ref, w1_ref, w3_ref, w2_ref, gf_ref, o_ref, *, final):
    x = x_ref[...]
    gates = _expert_gates(_router_logits(x, g_ref, wr_ref, br_ref))
    hb = _rmsnorm(x, g_ref[...]).astype(BF16)
    acc = x
    for e in range(N_EXPERTS):
        a = _dot(hb, w1_ref[e])
        hid = a * jax.nn.sigmoid(a) * _dot(hb, w3_ref[e])
        g = gates[:, COL_EXPERT0 + e:COL_EXPERT0 + e + 1]
        acc = acc + _dot((hid * g).astype(BF16), w2_ref[e])
    o_ref[...] = _rmsnorm(acc, gf_ref[...]) if final else acc


def _hier_moe(x, p, final_g, *, tm, final):
    shape = x.shape
    t = x.reshape(-1, D_MODEL)
    rows = t.shape[0]
    assert rows % tm == 0
    tile = pl.BlockSpec((tm, D_MODEL), lambda i: (i, 0))
    weights = [p['norm3_g'], p['w_router'], p['b_router'], p['w1'], p['w3'], p['w2'], final_g]
    out = pl.pallas_call(
        functools.partial(_moe_kernel, final=final),
        grid=(rows // tm,),
        in_specs=[tile] + [_resident(w.shape) for w in weights],
        out_specs=tile,
        out_shape=jax.ShapeDtypeStruct((rows, D_MODEL), F32),
        compiler_params=pltpu.CompilerParams(
            dimension_semantics=("parallel",), vmem_limit_bytes=VMEM_LIMIT_BYTES),
        name="hier_moe",
    )(t, *weights)
    return out.reshape(shape)


def _permute_kernel(pos_ref, src_ref, dst_ref, sem, *, rows, gather):
    def row_copy(j):
        p = pos_ref[0, 0, j]
        if gather:
            return pltpu.make_async_copy(src_ref.at[pl.ds(p, 1)], dst_ref.at[pl.ds(j, 1)], sem)
        return pltpu.make_async_copy(src_ref.at[pl.ds(j, 1)], dst_ref.at[pl.ds(p, 1)], sem)

    for j in range(rows):
        row_copy(j).start(priority=j % DMA_QUEUES)

    def wait(j, carry):
        row_copy(0).wait()
        return carry

    lax.fori_loop(0, rows, wait, 0, unroll=8)


def _permute_rows(src, pos, *, rows, gather):
    n, width = src.shape
    assert n % rows == 0
    steps = n // rows
    tile = pl.BlockSpec((rows, width), lambda s: (s, 0))
    hbm = pl.BlockSpec(memory_space=pl.ANY)
    return pl.pallas_call(
        functools.partial(_permute_kernel, rows=rows, gather=gather),
        grid=(steps,),
        in_specs=[pl.BlockSpec((1, 1, rows), lambda s: (s, 0, 0), memory_space=pltpu.SMEM),
                  hbm if gather else tile],
        out_specs=tile if gather else hbm,
        out_shape=jax.ShapeDtypeStruct((n, width), src.dtype),
        scratch_shapes=[pltpu.SemaphoreType.DMA(())],
        compiler_params=pltpu.CompilerParams(dimension_semantics=("arbitrary",)),
        name="gather_rows" if gather else "scatter_rows",
    )(pos.reshape(steps, 1, rows), src)


def _routing_plan(cls, rank, *, rank_tile, tm):
    n = cls.shape[0]
    tiles = n // tm
    rank_tiles = n // rank_tile
    onehot = cls.reshape(rank_tiles, rank_tile, 1) == jnp.arange(N_CLASSES, dtype=jnp.int32)
    counts = jnp.sum(onehot.astype(jnp.int32), axis=1)
    class_end = jnp.cumsum(jnp.sum(counts, axis=0))
    class_start = class_end - jnp.sum(counts, axis=0)
    tile_base = class_start[None, :] + jnp.cumsum(counts, axis=0) - counts
    pos = jnp.sum(jnp.where(onehot, tile_base[:, None, :], 0), axis=2).reshape(n) + rank
    tile_cut = jnp.arange(tiles, dtype=jnp.int32) * tm
    class_cut = class_start[1:]
    tile_slot = jnp.arange(tiles) + jnp.sum(class_cut[None, :] < tile_cut[:, None], axis=1)
    class_slot = (jnp.arange(N_CLASSES - 1)
                  + jnp.sum(tile_cut[None, :] <= class_cut[:, None], axis=1))
    slots = jnp.arange(tiles + N_CLASSES - 1)[:, None]
    lo = (jnp.sum(jnp.where(tile_slot[None, :] == slots, tile_cut[None, :], 0), axis=1)
          + jnp.sum(jnp.where(class_slot[None, :] == slots, class_cut[None, :], 0), axis=1))
    hi = jnp.concatenate([lo[1:], jnp.array([n], lo.dtype)])
    tile = jnp.minimum(lo // tm, tiles - 1)
    c = jnp.minimum(jnp.sum(class_end[None, :] <= lo[:, None], axis=1), N_CLASSES - 1)
    first = ((hi > lo) & (lo == tile * tm)).astype(jnp.int32)
    group, pair = c // PAIRS_PER_GROUP, c % PAIRS_PER_GROUP
    a = (pair >= EXPERTS_PER_GROUP - 1).astype(jnp.int32) + (pair >= 2 * EXPERTS_PER_GROUP - 3)
    b = pair - ((a * (2 * EXPERTS_PER_GROUP - 1 - a)) >> 1) + a + 1
    e_lo = group * EXPERTS_PER_GROUP + a
    e_hi = group * EXPERTS_PER_GROUP + b
    as_i32 = lambda a: a.astype(jnp.int32)
    return as_i32(pos), tuple(as_i32(a) for a in (tile, e_lo, e_hi, lo, hi, first))


def _moe_visit_kernel(tile_ref, elo_ref, ehi_ref, lo_ref, hi_ref, first_ref, xr_ref, g_ref,
                      w1a_ref, w1b_ref, w3a_ref, w3b_ref, w2a_ref, w2b_ref, gf_ref, o_ref, *,
                      tm, final):
    v = pl.program_id(0)
    lo, hi = lo_ref[v], hi_ref[v]

    @pl.when(hi > lo)
    def _():
        x = xr_ref[:, :D_MODEL]
        info = xr_ref[:, D_MODEL:]
        hb = _rmsnorm(x, g_ref[...]).astype(BF16)

        def hidden(w1_ref, w3_ref, gate):
            a = _dot(hb, w1_ref[0])
            return (a * jax.nn.sigmoid(a) * _dot(hb, w3_ref[0]) * gate).astype(BF16)

        h_lo = hidden(w1a_ref, w3a_ref, info[:, INFO_W_LO:INFO_W_LO + 1])
        h_hi = hidden(w1b_ref, w3b_ref, info[:, INFO_W_HI:INFO_W_HI + 1])
        res = x + (_dot(h_lo, w2a_ref[0]) + _dot(h_hi, w2b_ref[0]))
        if final:
            res = _rmsnorm(res, gf_ref[...])
        row = tile_ref[v] * tm + lax.broadcasted_iota(jnp.int32, (tm, 1), 0)
        mine = (row >= lo) & (row < hi)

        @pl.when(first_ref[v] == 1)
        def _():
            o_ref[...] = jnp.where(mine, res, 0.0)

        @pl.when(first_ref[v] == 0)
        def _():
            o_ref[...] = jnp.where(mine, res, o_ref[...])


def _moe_visits(xr_sorted, visits, p, final_g, *, tm, final):
    n = xr_sorted.shape[0]
    n_visits = visits[0].shape[0]
    by_tile = lambda width: pl.BlockSpec((tm, width), lambda v, tile, *_: (tile[v], 0))
    expert_lo = lambda shape: pl.BlockSpec((1,) + shape, lambda v, tile, elo, *_: (elo[v], 0, 0))
    expert_hi = lambda shape: pl.BlockSpec((1,) + shape,
                                           lambda v, tile, elo, ehi, *_: (ehi[v], 0, 0))
    up, down = (D_MODEL, D_EXPERT), (D_EXPERT, D_MODEL)
    return pl.pallas_call(
        functools.partial(_moe_visit_kernel, tm=tm, final=final),
        grid_spec=pltpu.PrefetchScalarGridSpec(
            num_scalar_prefetch=len(visits),
            grid=(n_visits,),
            in_specs=[by_tile(ROUTED_WIDTH), _resident(p['norm3_g'].shape),
                      expert_lo(up), expert_hi(up), expert_lo(up), expert_hi(up),
                      expert_lo(down), expert_hi(down), _resident(final_g.shape)],
            out_specs=by_tile(D_MODEL)),
        out_shape=jax.ShapeDtypeStruct((n, D_MODEL), F32),
        compiler_params=pltpu.CompilerParams(
            dimension_semantics=("arbitrary",), vmem_limit_bytes=VMEM_LIMIT_BYTES),
        name="moe_visits",
    )(*visits, xr_sorted, p['norm3_g'], p['w1'], p['w1'], p['w3'], p['w3'], p['w2'], p['w2'],
      final_g)


def _routed_moe(flat, p, final_g, *, rank_tile, tm, final):
    cls = flat[:, D_MODEL + INFO_CLASS].astype(jnp.int32)
    rank = flat[:, D_MODEL + INFO_RANK].astype(jnp.int32)
    pos, visits = _routing_plan(cls, rank, rank_tile=rank_tile, tm=tm)
    xr_sorted = _permute_rows(flat, pos, rows=PERMUTE_ROWS, gather=False)
    y_sorted = _moe_visits(xr_sorted, visits, p, final_g, tm=tm, final=final)
    return _permute_rows(y_sorted, pos, rows=PERMUTE_ROWS, gather=True)


def _split_bf16(w):
    hi = w.astype(BF16)
    lo = (w - hi.astype(F32)).astype(BF16)
    return jnp.concatenate([hi, lo], axis=1)


def _layer_params(l, norm1_g, w_in, conv_a_w, conv_a_b, w_a_out, ln_v_g, ln_v_b, w_s, b_s,
                  w_b_out, conv_c_w, conv_c_b, ln_c_g, ln_c_b, w_c_out, w_o, norm2_g,
                  w_q, w_mo, norm3_g, rg_w, rg_b, re_w, re_b, w1, w3, w2):
    row = lambda a: a[l].reshape(1, -1)
    bf = lambda a: a[l].astype(BF16)
    w_router = jnp.concatenate([rg_w[l], re_w[l].reshape(D_MODEL, N_EXPERTS)], axis=1)
    b_router = jnp.concatenate([rg_b[l], re_b[l].reshape(N_EXPERTS)])
    pad = ROUTER_COLS - MOE_GROUPS - N_EXPERTS
    return dict(
        norm1_g=row(norm1_g), w_in=bf(w_in), conv_a_w=conv_a_w[l], conv_a_b=row(conv_a_b),
        w_a_out=bf(w_a_out), ln_v_g=row(ln_v_g), ln_v_b=row(ln_v_b), w_s=w_s[l], b_s=b_s[l],
        w_b_out=bf(w_b_out), conv_c_w=conv_c_w[l], conv_c_b=row(conv_c_b), ln_c_g=row(ln_c_g),
        ln_c_b=row(ln_c_b), w_c_out=bf(w_c_out), w_o=bf(w_o), norm2_g=row(norm2_g),
        w_q=bf(w_q), w_mo=bf(w_mo),
        norm3_g=row(norm3_g), w_router=_split_bf16(jnp.pad(w_router, ((0, 0), (0, pad)))),
        b_router=jnp.pad(b_router, (0, pad)).reshape(1, -1),
        w1=bf(w1), w3=bf(w3), w2=bf(w2))


def kernel(x_prompt, x_sample, state_conv_a, state_conv_c, cache_mem_k, cache_mem_v, mem_prompt, norm1_g, w_in, conv_a_w, conv_a_b, w_a_out, ln_v_g, ln_v_b, w_s, b_s, w_b_out, conv_c_w, conv_c_b, ln_c_g, ln_c_b, w_c_out, w_o, norm2_g, mem_norm_g, w_q, w_k, w_v, w_mo, norm3_g, rg_w, rg_b, re_w, re_b, w1, w3, w2, final_norm_g):
    depth = w_in.shape[0]
    n_p, n_s = x_prompt.shape[0], x_sample.shape[0]
    final_g = final_norm_g.reshape(1, -1)
    x_p, x_s = x_prompt, x_sample
    zero_a = jnp.zeros((n_p, CONV_A - 1, D_A), F32)
    zero_c = jnp.zeros((n_p, CONV_C - 1, D_C), F32)
    sa_p, sa_s, sc_p, sc_s, v_s = [], [], [], [], []
    mk, mv, mk_heads, mv_heads = _memory_kv(mem_prompt, mem_norm_g, w_k, w_v, tm=512)
    mk = mk.reshape(depth, n_p, N_MEM, D_MODEL)
    mv = mv.reshape(depth, n_p, N_MEM, D_MODEL)
    ck = cache_mem_k.reshape(depth, n_s, N_MEM, D_MODEL)
    cv = cache_mem_v.reshape(depth, n_s, N_MEM, D_MODEL)
    for l in range(depth):
        p = _layer_params(l, norm1_g, w_in, conv_a_w, conv_a_b, w_a_out, ln_v_g, ln_v_b, w_s,
                          b_s, w_b_out, conv_c_w, conv_c_b, ln_c_g, ln_c_b, w_c_out, w_o,
                          norm2_g, w_q, w_mo, norm3_g, rg_w, rg_b, re_w,
                          re_b, w1, w3, w2)
        final = l == depth - 1
        x_p, na, nc = _token_mix(x_p, zero_a, zero_c, p, tl=512, emit_v=False)
        xr_p = _cross_attend(x_p, mk, mv, l, p, tl=512, route=True)
        x_p = _routed_moe(xr_p, p, final_g, rank_tile=512, tm=512, final=final).reshape(x_p.shape)
        sa_p.append(na)
        sc_p.append(nc)
        x_s, na, nc, vr = _token_mix(x_s, state_conv_a[l], state_conv_c[l], p, tl=x_s.shape[1],
                                     emit_v=True)
        x_s = _cross_attend(x_s, ck, cv, l, p, tl=x_s.shape[1], route=False)
        x_s = _hier_moe(x_s, p, final_g, tm=x_s.shape[0] * x_s.shape[1], final=final)
        sa_s.append(na)
        sc_s.append(nc)
        v_s.append(vr)
    return (x_p, x_s, jnp.stack(sa_p), jnp.stack(sa_s), jnp.stack(sc_p), jnp.stack(sc_s),
            jnp.stack(v_s), mk_heads.reshape(depth, n_p, N_MEM, MEM_HEADS, MEM_HEAD_DIM),
            mv_heads.reshape(depth, n_p, N_MEM, MEM_HEADS, MEM_HEAD_DIM))
```

```python
import functools

import jax
import jax.numpy as jnp
from jax import lax
from jax.experimental import pallas as pl
from jax.experimental.pallas import tpu as pltpu

D_MODEL = 1024
D_A = D_MODEL // 2
CONV_A = 3
D_B = D_MODEL // 2
B_GROUPS = 4
B_GROUP_DIM = D_B // B_GROUPS
B_CHUNK = 128
D_C = D_MODEL // 2
CONV_C = 31
N_MEM = 256
MEM_HEADS = 4
MEM_HEAD_DIM = D_MODEL // MEM_HEADS
MOE_GROUPS = 4
EXPERTS_PER_GROUP = 4
N_EXPERTS = MOE_GROUPS * EXPERTS_PER_GROUP
D_EXPERT = D_MODEL // 4
EPS = 1e-6
COL_XA = 0
COL_BA = COL_XA + D_A
COL_CA = COL_BA + D_A
COL_UV = COL_CA + D_A
COL_GLU = COL_UV + 2 * D_B
COL_GATE = COL_GLU + 2 * D_C
D_IN = COL_GATE + 3 * D_MODEL

SUBLANES = 8
LANES = 128
VMEM_LIMIT_BYTES = 56 * 1024 * 1024
DMA_QUEUES = 2

PA_NEW = SUBLANES
PA_HIST = PA_NEW - (CONV_A - 1)
PC_NEW = 32
PC_HIST = PC_NEW - (CONV_C - 1)
CONV_ROWS = 32
ROUTER_COLS = LANES
COL_EXPERT0 = MOE_GROUPS
ROUTE_ROWS = 32
PAIRS_PER_GROUP = EXPERTS_PER_GROUP * (EXPERTS_PER_GROUP - 1) // 2
N_CLASSES = MOE_GROUPS * PAIRS_PER_GROUP
INFO_CLASS, INFO_RANK, INFO_W_LO, INFO_W_HI = 0, 1, 2, 3
ROUTED_WIDTH = D_MODEL + ROUTER_COLS
PERMUTE_ROWS = 2048

BF16 = jnp.bfloat16
F32 = jnp.float32


def _dot(a, b):
    return jnp.dot(a, b, preferred_element_type=F32)


def _rmsnorm(x, g):
    ms = jnp.mean(x * x, axis=-1, keepdims=True)
    return x * lax.rsqrt(ms + EPS) * g


def _layernorm(x, g, b):
    xc = x - jnp.mean(x, axis=-1, keepdims=True)
    var = jnp.mean(xc * xc, axis=-1, keepdims=True)
    return xc * lax.rsqrt(var + EPS) * g + b


def _resident(shape):
    nd = len(shape)
    return pl.BlockSpec(shape, lambda *_: (0,) * nd, pipeline_mode=pl.Buffered(1))


def _mix_kernel(x_ref, sa_ref, sc_ref, g1_ref, win_ref, caw_ref, cab_ref, wa_ref, lvg_ref,
                lvb_ref, ws_ref, bs_ref, wb_ref, ccw_ref, ccb_ref, lcg_ref, lcb_ref, wc_ref,
                wo_ref, xo_ref, na_ref, nc_ref, *rest, tl, cl, emit_v):
    v_ref = rest[0] if emit_v else None
    pa_buf, pc_buf, s_buf, act_buf = rest[-4:]
    t = pl.program_id(1)

    @pl.when(t == 0)
    def _():
        pa_buf[PA_HIST:PA_NEW, :] = sa_ref[0]
        pc_buf[PC_HIST:PC_NEW, :] = sc_ref[0]

    x = x_ref[0]
    hb = _rmsnorm(x, g1_ref[...]).astype(BF16)

    def gate(i):
        lo = COL_GATE + i * D_MODEL
        return jax.nn.sigmoid(_dot(hb, win_ref[:, lo:lo + D_MODEL]))

    zg = _dot(hb, win_ref[:, COL_GLU:COL_GATE])
    pc_buf[PC_NEW:PC_NEW + tl, :] = zg[:, :D_C] * jax.nn.sigmoid(zg[:, D_C:])
    for r in range(tl // CONV_ROWS):
        base = PC_HIST + r * CONV_ROWS
        acc = jnp.broadcast_to(ccb_ref[...], (CONV_ROWS, D_C))
        for off in range(SUBLANES):
            part = None
            for k in range(CONV_C):
                if (base + k) % SUBLANES == off:
                    term = ccw_ref[k:k + 1, :] * pc_buf[base + k:base + k + CONV_ROWS, :]
                    part = term if part is None else part + term
            acc = acc + part
        ln = _layernorm(acc, lcg_ref[...], lcb_ref[...])
        act_buf[r * CONV_ROWS:(r + 1) * CONV_ROWS, :] = (ln * jax.nn.sigmoid(ln)).astype(BF16)
    new_c = pc_buf[PC_HIST + tl:PC_NEW + tl, :]
    nc_ref[0] = new_c
    pc_buf[PC_HIST:PC_NEW, :] = new_c

    za = _dot(hb, win_ref[:, COL_XA:COL_UV])
    pa_buf[PA_NEW:PA_NEW + tl, :] = za[:, 2 * D_A:] * za[:, :D_A]
    conv_a = cab_ref[...]
    for k in range(CONV_A):
        conv_a = conv_a + caw_ref[k:k + 1, :] * pa_buf[PA_HIST + k:PA_HIST + k + tl, :]
    y_a = _dot((za[:, D_A:2 * D_A] * conv_a).astype(BF16), wa_ref[...])
    merged = gate(0) * y_a
    new_a = pa_buf[PA_HIST + tl:PA_NEW + tl, :]
    na_ref[0] = new_a
    pa_buf[PA_HIST:PA_NEW, :] = new_a

    zuv = _dot(hb, win_ref[:, COL_UV:COL_GLU])
    uv = 0.5 * zuv * (1.0 + lax.erf(zuv * (0.5 ** 0.5)))
    v = _layernorm(uv[:, D_B:], lvg_ref[...], lvb_ref[...])
    if emit_v:
        v_ref[0] = v
    vb = v.astype(BF16)
    row = lax.broadcasted_iota(jnp.int32, (cl, cl), 0)
    col = lax.broadcasted_iota(jnp.int32, (cl, cl), 1)
    for g in range(B_GROUPS):
        wg = jnp.where(row >= col, ws_ref[g], 0.0).astype(BF16)
        for c in range(tl // cl):
            blk = vb[c * cl:(c + 1) * cl, g * B_GROUP_DIM:(g + 1) * B_GROUP_DIM]
            s_buf[c * cl:(c + 1) * cl, g * B_GROUP_DIM:(g + 1) * B_GROUP_DIM] = (
                _dot(wg, blk) + bs_ref[:, g * B_GROUP_DIM:(g + 1) * B_GROUP_DIM])
    y_b = _dot((uv[:, :D_B] * s_buf[...]).astype(BF16), wb_ref[...])
    merged = merged + gate(1) * y_b

    y_c = _dot(act_buf[...], wc_ref[...])
    merged = merged + gate(2) * y_c

    xo_ref[0] = x + _dot(merged.astype(BF16), wo_ref[...])


def _token_mix(x, state_a, state_c, p, *, tl, emit_v):
    n, L, _ = x.shape
    cl = min(L, B_CHUNK)
    assert L % tl == 0 and tl % cl == 0 and tl % CONV_ROWS == 0 and tl >= CONV_C - 1
    ws = p['w_s'][:, :cl, :cl]
    bs = jnp.repeat(p['b_s'][:, :cl].T, B_GROUP_DIM, axis=1)
    weights = [p['norm1_g'], p['w_in'], p['conv_a_w'], p['conv_a_b'], p['w_a_out'],
               p['ln_v_g'], p['ln_v_b'], ws, bs, p['w_b_out'], p['conv_c_w'], p['conv_c_b'],
               p['ln_c_g'], p['ln_c_b'], p['w_c_out'], p['w_o']]
    tile = lambda width: pl.BlockSpec((1, tl, width), lambda b, t: (b, t, 0))
    per_seq = lambda rows, width: pl.BlockSpec((1, rows, width), lambda b, t: (b, 0, 0))
    return pl.pallas_call(
        functools.partial(_mix_kernel, tl=tl, cl=cl, emit_v=emit_v),
        grid=(n, L // tl),
        in_specs=[tile(D_MODEL), per_seq(CONV_A - 1, D_A), per_seq(CONV_C - 1, D_C)]
        + [_resident(w.shape) for w in weights],
        out_specs=[tile(D_MODEL), per_seq(CONV_A - 1, D_A), per_seq(CONV_C - 1, D_C)]
        + ([tile(D_B)] if emit_v else []),
        out_shape=[jax.ShapeDtypeStruct((n, L, D_MODEL), F32),
                   jax.ShapeDtypeStruct((n, CONV_A - 1, D_A), F32),
                   jax.ShapeDtypeStruct((n, CONV_C - 1, D_C), F32)]
        + ([jax.ShapeDtypeStruct((n, L, D_B), F32)] if emit_v else []),
        scratch_shapes=[pltpu.VMEM((PA_NEW + tl, D_A), F32),
                        pltpu.VMEM((PC_NEW + tl, D_C), F32),
                        pltpu.VMEM((tl, D_B), F32),
                        pltpu.VMEM((tl, D_C), BF16)],
        compiler_params=pltpu.CompilerParams(
            dimension_semantics=("parallel", "arbitrary"), vmem_limit_bytes=VMEM_LIMIT_BYTES),
        name="token_mix",
    )(x, state_a, state_c, *weights)


def _memkv_kernel(m_ref, g_ref, wk_ref, wv_ref, kf_ref, vf_ref, kh_ref, vh_ref):
    mb = _rmsnorm(m_ref[...], g_ref[0]).astype(BF16)
    for w_ref, flat_ref, heads_ref in ((wk_ref, kf_ref, kh_ref), (wv_ref, vf_ref, vh_ref)):
        proj = _dot(mb, w_ref[0])
        flat_ref[0] = proj
        for h in range(MEM_HEADS):
            heads_ref[0, :, h, :] = proj[:, h * MEM_HEAD_DIM:(h + 1) * MEM_HEAD_DIM]


def _memory_kv(mem, mem_norm_g, w_k, w_v, *, tm):
    depth = w_k.shape[0]
    n, m, _ = mem.shape
    rows = n * m
    assert rows % tm == 0
    flat = pl.BlockSpec((1, tm, D_MODEL), lambda l, i: (l, i, 0))
    heads = pl.BlockSpec((1, tm, MEM_HEADS, MEM_HEAD_DIM), lambda l, i: (l, i, 0, 0))
    weight = pl.BlockSpec((1, D_MODEL, D_MODEL), lambda l, i: (l, 0, 0))
    return pl.pallas_call(
        _memkv_kernel,
        grid=(depth, rows // tm),
        in_specs=[pl.BlockSpec((tm, D_MODEL), lambda l, i: (i, 0)),
                  pl.BlockSpec((1, 1, D_MODEL), lambda l, i: (l, 0, 0)), weight, weight],
        out_specs=[flat, flat, heads, heads],
        out_shape=[jax.ShapeDtypeStruct((depth, rows, D_MODEL), F32)] * 2
        + [jax.ShapeDtypeStruct((depth, rows, MEM_HEADS, MEM_HEAD_DIM), F32)] * 2,
        compiler_params=pltpu.CompilerParams(
            dimension_semantics=("parallel", "parallel"), vmem_limit_bytes=VMEM_LIMIT_BYTES),
        name="memory_kv",
    )(mem.reshape(rows, D_MODEL), mem_norm_g.reshape(depth, 1, D_MODEL), w_k.astype(BF16),
      w_v.astype(BF16))


def _attention(x, k_ref, v_ref, g_ref, wq_ref, wmo_ref, o_buf):
    hb = _rmsnorm(x, g_ref[...]).astype(BF16)
    qb = (_dot(hb, wq_ref[...]) * (MEM_HEAD_DIM ** -0.5)).astype(BF16)
    heads = [slice(h * MEM_HEAD_DIM, (h + 1) * MEM_HEAD_DIM) for h in range(MEM_HEADS)]
    scores = [lax.dot_general(qb[:, cols], k_ref[0, 0, :, cols].astype(BF16),
                              (((1,), (1,)), ((), ())), preferred_element_type=F32)
              for cols in heads]
    for s, cols in zip(scores, heads):
        e = jnp.exp(s - jnp.max(s, axis=-1, keepdims=True))
        prob = e * (1.0 / jnp.sum(e, axis=-1, keepdims=True))
        o_buf[:, cols] = _dot(prob.astype(BF16), v_ref[0, 0, :, cols].astype(BF16)).astype(BF16)
    return x + _dot(o_buf[...], wmo_ref[...])


def _attn_kernel(x_ref, k_ref, v_ref, g_ref, wq_ref, wmo_ref, *rest, route):
    xo_ref, o_buf = rest[-2:]
    xn = _attention(x_ref[0], k_ref, v_ref, g_ref, wq_ref, wmo_ref, o_buf)
    if route:
        g3_ref, wr_ref, br_ref = rest[:3]
        xo_ref[:, :D_MODEL] = xn
        xo_ref[:, D_MODEL:] = _route_info(xn, g3_ref, wr_ref, br_ref)
    else:
        xo_ref[0] = xn


def _cross_attend(x, mem_k, mem_v, layer, p, *, tl, route):
    n, L, _ = x.shape
    assert L % tl == 0
    per_seq_tiles = L // tl
    tile = pl.BlockSpec((1, tl, D_MODEL), lambda b, t: (b, t, 0))
    per_seq = pl.BlockSpec((1, 1, N_MEM, D_MODEL), lambda b, t: (layer, b, 0, 0))
    weights = [p['norm2_g'], p['w_q'], p['w_mo']]
    if route:
        weights += [p['norm3_g'], p['w_router_t'], p['b_router_t']]
        out_spec = pl.BlockSpec((tl, ROUTED_WIDTH), lambda b, t: (b * per_seq_tiles + t, 0))
        out_shape = jax.ShapeDtypeStruct((n * L, ROUTED_WIDTH), F32)
    else:
        out_spec, out_shape = tile, jax.ShapeDtypeStruct((n, L, D_MODEL), F32)
    return pl.pallas_call(
        functools.partial(_attn_kernel, route=route),
        grid=(n, per_seq_tiles),
        in_specs=[tile, per_seq, per_seq] + [_resident(w.shape) for w in weights],
        out_specs=out_spec,
        out_shape=out_shape,
        scratch_shapes=[pltpu.VMEM((tl, D_MODEL), BF16)],
        compiler_params=pltpu.CompilerParams(
            dimension_semantics=("parallel", "arbitrary"), vmem_limit_bytes=VMEM_LIMIT_BYTES),
        name="cross_attend",
    )(x, mem_k, mem_v, *weights)


def _router_logits(x, g_ref, wr_ref, br_ref):
    h = _rmsnorm(x, g_ref[...])
    h_hi = h.astype(BF16)
    h_lo = (h - h_hi.astype(F32)).astype(BF16)
    by_hi = _dot(h_hi, wr_ref[...])
    by_lo = _dot(h_lo, wr_ref[:, :ROUTER_COLS])
    return by_hi[:, :ROUTER_COLS] + (by_hi[:, ROUTER_COLS:] + by_lo) + br_ref[...]


def _route(logits, axis):
    idx = lax.broadcasted_iota(jnp.int32, logits.shape, axis)
    along = lambda f, a: f(a, axis=axis, keepdims=True)
    neg = -jnp.inf
    big = jnp.int32(logits.shape[axis])
    lg = jnp.where(idx < MOE_GROUPS, logits, neg)
    mg = along(jnp.max, lg)
    p_top = 1.0 / along(jnp.sum, jnp.exp(lg - mg))
    g_idx = along(jnp.min, jnp.where(lg == mg, idx, big))
    expert = idx - COL_EXPERT0
    in_group = (expert >= 0) & (expert < N_EXPERTS) & ((expert >> 2) == g_idx)
    le = jnp.where(in_group, logits, neg)
    m1 = along(jnp.max, le)
    i1 = along(jnp.min, jnp.where(le == m1, idx, big))
    le2 = jnp.where(idx == i1, neg, le)
    m2 = along(jnp.max, le2)
    i2 = along(jnp.min, jnp.where(le2 == m2, idx, big))
    e2 = jnp.exp(m2 - m1)
    w1 = p_top / (1.0 + e2)
    w2 = p_top * e2 / (1.0 + e2)
    return idx, g_idx, i1, i2, w1, w2


def _expert_gates(logits):
    col, _, i1, i2, w1, w2 = _route(logits, 1)
    return jnp.where(col == i1, w1, 0.0) + jnp.where(col == i2, w2, 0.0)


def _route_info(x, g_ref, wrt_ref, brt_ref):
    tm = x.shape[0]
    h = _rmsnorm(x, g_ref[...])
    h_hi = h.astype(BF16)
    h_lo = (h - h_hi.astype(F32)).astype(BF16)
    contract_last = (((1,), (1,)), ((), ()))
    by_hi = lax.dot_general(wrt_ref[...], h_hi, contract_last, preferred_element_type=F32)
    by_lo = lax.dot_general(wrt_ref[:ROUTER_COLS, :], h_lo, contract_last,
                            preferred_element_type=F32)
    logits = by_hi[:ROUTER_COLS] + (by_hi[ROUTER_COLS:] + by_lo)
    logits = logits[:ROUTE_ROWS] + brt_ref[...]

    _, g_idx, i1, i2, w1, w2 = _route(logits, 0)
    first_is_lo = i1 < i2
    a = (jnp.minimum(i1, i2) - COL_EXPERT0) & (EXPERTS_PER_GROUP - 1)
    b = (jnp.maximum(i1, i2) - COL_EXPERT0) & (EXPERTS_PER_GROUP - 1)
    cls = g_idx * PAIRS_PER_GROUP + ((a * (2 * EXPERTS_PER_GROUP - 1 - a)) >> 1) + (b - a - 1)
    class_row = lax.broadcasted_iota(jnp.int32, (ROUTE_ROWS, tm), 0)
    onehot = jnp.where(class_row == cls, 1.0, 0.0)
    earlier = lax.broadcasted_iota(jnp.int32, (tm, tm), 0)
    token = lax.broadcasted_iota(jnp.int32, (tm, tm), 1)
    upto = jnp.where(earlier <= token, 1.0, 0.0).astype(BF16)
    running = _dot(onehot.astype(BF16), upto)
    rank = jnp.sum(running * onehot, axis=0, keepdims=True) - 1.0
    w_lo = jnp.where(first_is_lo, w1, w2)
    w_hi = jnp.where(first_is_lo, w2, w1)
    field = lax.broadcasted_iota(jnp.int32, (ROUTER_COLS, tm), 0)
    record = jnp.where(field == INFO_CLASS, cls.astype(F32),
                       jnp.where(field == INFO_RANK, rank,
                                 jnp.where(field == INFO_W_LO, w_lo,
                                           jnp.where(field == INFO_W_HI, w_hi, 0.0))))
    return record.T


def _moe_kernel(x_ref, g_ref, wr_ref, br_ref, w1_ref, w3_ref, w2_ref, gf_ref, o_ref, *, final):
    x = x_ref[...]
    gates = _expert_gates(_router_logits(x, g_ref, wr_ref, br_ref))
    hb = _rmsnorm(x, g_ref[...]).astype(BF16)
    acc = x
    for e in range(N_EXPERTS):
        a = _dot(hb, w1_ref[e])
        hid = a * jax.nn.sigmoid(a) * _dot(hb, w3_ref[e])
        g = gates[:, COL_EXPERT0 + e:COL_EXPERT0 + e + 1]
        acc = acc + _dot((hid * g).astype(BF16), w2_ref[e])
    o_ref[...] = _rmsnorm(acc, gf_ref[...]) if final else acc


def _hier_moe(x, p, final_g, *, tm, final):
    shape = x.shape
    t = x.reshape(-1, D_MODEL)
    rows = t.shape[0]
    assert rows % tm == 0
    tile = pl.BlockSpec((tm, D_MODEL), lambda i: (i, 0))
    weights = [p['norm3_g'], p['w_router'], p['b_router'], p['w1'], p['w3'], p['w2'], final_g]
    out = pl.pallas_call(
        functools.partial(_moe_kernel, final=final),
        grid=(rows // tm,),
        in_specs=[tile] + [_resident(w.shape) for w in weights],
        out_specs=tile,
        out_shape=jax.ShapeDtypeStruct((rows, D_MODEL), F32),
        compiler_params=pltpu.CompilerParams(
            dimension_semantics=("parallel",), vmem_limit_bytes=VMEM_LIMIT_BYTES),
        name="hier_moe",
    )(t, *weights)
    return out.reshape(shape)


def _permute_kernel(pos_ref, src_ref, dst_ref, sem, *, rows, gather):
    def row_copy(j):
        p = pos_ref[0, 0, j]
        if gather:
            return pltpu.make_async_copy(src_ref.at[pl.ds(p, 1)], dst_ref.at[pl.ds(j, 1)], sem)
        return pltpu.make_async_copy(src_ref.at[pl.ds(j, 1)], dst_ref.at[pl.ds(p, 1)], sem)

    for j in range(rows):
        row_copy(j).start(priority=j % DMA_QUEUES)

    def wait(j, carry):
        row_copy(0).wait()
        return carry

    lax.fori_loop(0, rows, wait, 0, unroll=8)


def _permute_rows(src, pos, *, rows, gather):
    n, width = src.shape
    assert n % rows == 0
    steps = n // rows
    tile = pl.BlockSpec((rows, width), lambda s: (s, 0))
    hbm = pl.BlockSpec(memory_space=pl.ANY)
    return pl.pallas_call(
        functools.partial(_permute_kernel, rows=rows, gather=gather),
        grid=(steps,),
        in_specs=[pl.BlockSpec((1, 1, rows), lambda s: (s, 0, 0), memory_space=pltpu.SMEM),
                  hbm if gather else tile],
        out_specs=tile if gather else hbm,
        out_shape=jax.ShapeDtypeStruct((n, width), src.dtype),
        scratch_shapes=[pltpu.SemaphoreType.DMA(())],
        compiler_params=pltpu.CompilerParams(dimension_semantics=("arbitrary",)),
        name="gather_rows" if gather else "scatter_rows",
    )(pos.reshape(steps, 1, rows), src)


def _routing_plan(cls, rank, *, rank_tile, tm):
    n = cls.shape[0]
    tiles = n // tm
    rank_tiles = n // rank_tile
    onehot = cls.reshape(rank_tiles, rank_tile, 1) == jnp.arange(N_CLASSES, dtype=jnp.int32)
    counts = jnp.sum(onehot.astype(jnp.int32), axis=1)
    class_end = jnp.cumsum(jnp.sum(counts, axis=0))
    class_start = class_end - jnp.sum(counts, axis=0)
    tile_base = class_start[None, :] + jnp.cumsum(counts, axis=0) - counts
    pos = jnp.sum(jnp.where(onehot, tile_base[:, None, :], 0), axis=2).reshape(n) + rank
    tile_cut = jnp.arange(tiles, dtype=jnp.int32) * tm
    class_cut = class_start[1:]
    tile_slot = jnp.arange(tiles) + jnp.sum(class_cut[None, :] < tile_cut[:, None], axis=1)
    class_slot = (jnp.arange(N_CLASSES - 1)
                  + jnp.sum(tile_cut[None, :] <= class_cut[:, None], axis=1))
    slots = jnp.arange(tiles + N_CLASSES - 1)[:, None]
    lo = (jnp.sum(jnp.where(tile_slot[None, :] == slots, tile_cut[None, :], 0), axis=1)
          + jnp.sum(jnp.where(class_slot[None, :] == slots, class_cut[None, :], 0), axis=1))
    hi = jnp.concatenate([lo[1:], jnp.array([n], lo.dtype)])
    tile = jnp.minimum(lo // tm, tiles - 1)
    c = jnp.minimum(jnp.sum(class_end[None, :] <= lo[:, None], axis=1), N_CLASSES - 1)
    first = ((hi > lo) & (lo == tile * tm)).astype(jnp.int32)
    group, pair = c // PAIRS_PER_GROUP, c % PAIRS_PER_GROUP
    a = (pair >= EXPERTS_PER_GROUP - 1).astype(jnp.int32) + (pair >= 2 * EXPERTS_PER_GROUP - 3)
    b = pair - ((a * (2 * EXPERTS_PER_GROUP - 1 - a)) >> 1) + a + 1
    e_lo = group * EXPERTS_PER_GROUP + a
    e_hi = group * EXPERTS_PER_GROUP + b
    as_i32 = lambda a: a.astype(jnp.int32)
    return as_i32(pos), tuple(as_i32(a) for a in (tile, e_lo, e_hi, lo, hi, first))


def _moe_visit_kernel(tile_ref, elo_ref, ehi_ref, lo_ref, hi_ref, first_ref, xr_ref, g_ref,
                      w1a_ref, w1b_ref, w3a_ref, w3b_ref, w2a_ref, w2b_ref, gf_ref, o_ref, *,
                      tm, final):
    v = pl.program_id(0)
    lo, hi = lo_ref[v], hi_ref[v]

    @pl.when(hi > lo)
    def _():
        x = xr_ref[:, :D_MODEL]
        info = xr_ref[:, D_MODEL:]
        hb = _rmsnorm(x, g_ref[...]).astype(BF16)

        def hidden(w1_ref, w3_ref, gate):
            a = _dot(hb, w1_ref[0])
            return (a * jax.nn.sigmoid(a) * _dot(hb, w3_ref[0]) * gate).astype(BF16)

        h_lo = hidden(w1a_ref, w3a_ref, info[:, INFO_W_LO:INFO_W_LO + 1])
        h_hi = hidden(w1b_ref, w3b_ref, info[:, INFO_W_HI:INFO_W_HI + 1])
        res = x + (_dot(h_lo, w2a_ref[0]) + _dot(h_hi, w2b_ref[0]))
        if final:
            res = _rmsnorm(res, gf_ref[...])
        row = tile_ref[v] * tm + lax.broadcasted_iota(jnp.int32, (tm, 1), 0)
        mine = (row >= lo) & (row < hi)

        @pl.when(first_ref[v] == 1)
        def _():
            o_ref[...] = jnp.where(mine, res, 0.0)

        @pl.when(first_ref[v] == 0)
        def _():
            o_ref[...] = jnp.where(mine, res, o_ref[...])


def _moe_visits(xr_sorted, visits, p, final_g, *, tm, final):
    n = xr_sorted.shape[0]
    n_visits = visits[0].shape[0]
    by_tile = lambda width: pl.BlockSpec((tm, width), lambda v, tile, *_: (tile[v], 0))
    expert_lo = lambda shape: pl.BlockSpec((1,) + shape, lambda v, tile, elo, *_: (elo[v], 0, 0))
    expert_hi = lambda shape: pl.BlockSpec((1,) + shape,
                                           lambda v, tile, elo, ehi, *_: (ehi[v], 0, 0))
    up, down = (D_MODEL, D_EXPERT), (D_EXPERT, D_MODEL)
    return pl.pallas_call(
        functools.partial(_moe_visit_kernel, tm=tm, final=final),
        grid_spec=pltpu.PrefetchScalarGridSpec(
            num_scalar_prefetch=len(visits),
            grid=(n_visits,),
            in_specs=[by_tile(ROUTED_WIDTH), _resident(p['norm3_g'].shape),
                      expert_lo(up), expert_hi(up), expert_lo(up), expert_hi(up),
                      expert_lo(down), expert_hi(down), _resident(final_g.shape)],
            out_specs=by_tile(D_MODEL)),
        out_shape=jax.ShapeDtypeStruct((n, D_MODEL), F32),
        compiler_params=pltpu.CompilerParams(
            dimension_semantics=("arbitrary",), vmem_limit_bytes=VMEM_LIMIT_BYTES),
        name="moe_visits",
    )(*visits, xr_sorted, p['norm3_g'], p['w1'], p['w1'], p['w3'], p['w3'], p['w2'], p['w2'],
      final_g)


def _routed_moe(flat, p, final_g, *, rank_tile, tm, final):
    cls = flat[:, D_MODEL + INFO_CLASS].astype(jnp.int32)
    rank = flat[:, D_MODEL + INFO_RANK].astype(jnp.int32)
    pos, visits = _routing_plan(cls, rank, rank_tile=rank_tile, tm=tm)
    xr_sorted = _permute_rows(flat, pos, rows=PERMUTE_ROWS, gather=False)
    y_sorted = _moe_visits(xr_sorted, visits, p, final_g, tm=tm, final=final)
    return _permute_rows(y_sorted, pos, rows=PERMUTE_ROWS, gather=True)


def _split_bf16(w):
    hi = w.astype(BF16)
    lo = (w - hi.astype(F32)).astype(BF16)
    return jnp.concatenate([hi, lo], axis=1)


def _layer_params(l, norm1_g, w_in, conv_a_w, conv_a_b, w_a_out, ln_v_g, ln_v_b, w_s, b_s,
                  w_b_out, conv_c_w, conv_c_b, ln_c_g, ln_c_b, w_c_out, w_o, norm2_g,
                  w_q, w_mo, norm3_g, rg_w, rg_b, re_w, re_b, w1, w3, w2):
    row = lambda a: a[l].reshape(1, -1)
    bf = lambda a: a[l].astype(BF16)
    w_router = jnp.concatenate([rg_w[l], re_w[l].reshape(D_MODEL, N_EXPERTS)], axis=1)
    b_router = jnp.concatenate([rg_b[l], re_b[l].reshape(N_EXPERTS)])
    pad = ROUTER_COLS - MOE_GROUPS - N_EXPERTS
    return dict(
        norm1_g=row(norm1_g), w_in=bf(w_in), conv_a_w=conv_a_w[l], conv_a_b=row(conv_a_b),
        w_a_out=bf(w_a_out), ln_v_g=row(ln_v_g), ln_v_b=row(ln_v_b), w_s=w_s[l], b_s=b_s[l],
        w_b_out=bf(w_b_out), conv_c_w=conv_c_w[l], conv_c_b=row(conv_c_b), ln_c_g=row(ln_c_g),
        ln_c_b=row(ln_c_b), w_c_out=bf(w_c_out), w_o=bf(w_o), norm2_g=row(norm2_g),
        w_q=bf(w_q), w_mo=bf(w_mo),
        norm3_g=row(norm3_g), w_router=_split_bf16(jnp.pad(w_router, ((0, 0), (0, pad)))),
        b_router=jnp.pad(b_router, (0, pad)).reshape(1, -1),
        w_router_t=_split_bf16(jnp.pad(w_router, ((0, 0), (0, pad)))).T,
        b_router_t=jnp.pad(b_router, (0, ROUTE_ROWS - MOE_GROUPS - N_EXPERTS)).reshape(-1, 1),
        w1=bf(w1), w3=bf(w3), w2=bf(w2))


def kernel(x_prompt, x_sample, state_conv_a, state_conv_c, cache_mem_k, cache_mem_v, mem_prompt, norm1_g, w_in, conv_a_w, conv_a_b, w_a_out, ln_v_g, ln_v_b, w_s, b_s, w_b_out, conv_c_w, conv_c_b, ln_c_g, ln_c_b, w_c_out, w_o, norm2_g, mem_norm_g, w_q, w_k, w_v, w_mo, norm3_g, rg_w, rg_b, re_w, re_b, w1, w3, w2, final_norm_g):
    depth = w_in.shape[0]
    n_p, n_s = x_prompt.shape[0], x_sample.shape[0]
    final_g = final_norm_g.reshape(1, -1)
    x_p, x_s = x_prompt, x_sample
    zero_a = jnp.zeros((n_p, CONV_A - 1, D_A), F32)
    zero_c = jnp.zeros((n_p, CONV_C - 1, D_C), F32)
    sa_p, sa_s, sc_p, sc_s, v_s = [], [], [], [], []
    mk, mv, mk_heads, mv_heads = _memory_kv(mem_prompt, mem_norm_g, w_k, w_v, tm=512)
    mk = mk.reshape(depth, n_p, N_MEM, D_MODEL)
    mv = mv.reshape(depth, n_p, N_MEM, D_MODEL)
    ck = cache_mem_k.reshape(depth, n_s, N_MEM, D_MODEL)
    cv = cache_mem_v.reshape(depth, n_s, N_MEM, D_MODEL)
    for l in range(depth):
        p = _layer_params(l, norm1_g, w_in, conv_a_w, conv_a_b, w_a_out, ln_v_g, ln_v_b, w_s,
                          b_s, w_b_out, conv_c_w, conv_c_b, ln_c_g, ln_c_b, w_c_out, w_o,
                          norm2_g, w_q, w_mo, norm3_g, rg_w, rg_b, re_w,
                          re_b, w1, w3, w2)
        final = l == depth - 1
        x_p, na, nc = _token_mix(x_p, zero_a, zero_c, p, tl=512, emit_v=False)
        xr_p = _cross_attend(x_p, mk, mv, l, p, tl=512, route=True)
        x_p = _routed_moe(xr_p, p, final_g, rank_tile=512, tm=512, final=final).reshape(x_p.shape)
        sa_p.append(na)
        sc_p.append(nc)
        x_s, na, nc, vr = _token_mix(x_s, state_conv_a[l], state_conv_c[l], p, tl=x_s.shape[1],
                                     emit_v=True)
        x_s = _cross_attend(x_s, ck, cv, l, p, tl=x_s.shape[1], route=False)
        x_s = _hier_moe(x_s, p, final_g, tm=x_s.shape[0] * x_s.shape[1], final=final)
        sa_s.append(na)
        sc_s.append(nc)
        v_s.append(vr)
    return (x_p, x_s, jnp.stack(sa_p), jnp.stack(sa_s), jnp.stack(sc_p), jnp.stack(sc_s),
            jnp.stack(v_s), mk_heads.reshape(depth, n_p, N_MEM, MEM_HEADS, MEM_HEAD_DIM),
            mv_heads.reshape(depth, n_p, N_MEM, MEM_HEADS, MEM_HEAD_DIM))
```

```python
import functools

import jax
import jax.numpy as jnp
from jax import lax
from jax.experimental import pallas as pl
from jax.experimental.pallas import tpu as pltpu

D_MODEL = 1024
D_A = D_MODEL // 2
CONV_A = 3
D_B = D_MODEL // 2
B_GROUPS = 4
B_GROUP_DIM = D_B // B_GROUPS
B_CHUNK = 128
D_C = D_MODEL // 2
CONV_C = 31
N_MEM = 256
MEM_HEADS = 4
MEM_HEAD_DIM = D_MODEL // MEM_HEADS
MOE_GROUPS = 4
EXPERTS_PER_GROUP = 4
N_EXPERTS = MOE_GROUPS * EXPERTS_PER_GROUP
D_EXPERT = D_MODEL // 4
EPS = 1e-6
COL_XA = 0
COL_BA = COL_XA + D_A
COL_CA = COL_BA + D_A
COL_UV = COL_CA + D_A
COL_GLU = COL_UV + 2 * D_B
COL_GATE = COL_GLU + 2 * D_C
D_IN = COL_GATE + 3 * D_MODEL

SUBLANES = 8
LANES = 128
VMEM_LIMIT_BYTES = 56 * 1024 * 1024
DMA_QUEUES = 2

PA_NEW = SUBLANES
PA_HIST = PA_NEW - (CONV_A - 1)
PC_NEW = 32
PC_HIST = PC_NEW - (CONV_C - 1)
CONV_ROWS = 64
ROUTER_COLS = LANES
COL_EXPERT0 = MOE_GROUPS
ROUTE_ROWS = 32
PAIRS_PER_GROUP = EXPERTS_PER_GROUP * (EXPERTS_PER_GROUP - 1) // 2
N_CLASSES = MOE_GROUPS * PAIRS_PER_GROUP
INFO_CLASS, INFO_RANK, INFO_W_LO, INFO_W_HI = 0, 1, 2, 3
ROUTED_WIDTH = D_MODEL + ROUTER_COLS
PERMUTE_ROWS = 1024

BF16 = jnp.bfloat16
F32 = jnp.float32


def _dot(a, b):
    return jnp.dot(a, b, preferred_element_type=F32)


def _rmsnorm(x, g):
    ms = jnp.mean(x * x, axis=-1, keepdims=True)
    return x * lax.rsqrt(ms + EPS) * g


def _layernorm(x, g, b):
    xc = x - jnp.mean(x, axis=-1, keepdims=True)
    var = jnp.mean(xc * xc, axis=-1, keepdims=True)
    return xc * lax.rsqrt(var + EPS) * g + b


def _resident(shape):
    nd = len(shape)
    return pl.BlockSpec(shape, lambda *_: (0,) * nd, pipeline_mode=pl.Buffered(1))


def _mix_kernel(x_ref, sa_ref, sc_ref, g1_ref, win_ref, caw_ref, cab_ref, wa_ref, lvg_ref,
                lvb_ref, ws_ref, bs_ref, wb_ref, ccw_ref, ccb_ref, lcg_ref, lcb_ref, wc_ref,
                wo_ref, xo_ref, na_ref, nc_ref, *rest, tl, cl, emit_v):
    v_ref = rest[0] if emit_v else None
    pa_buf, pc_buf, s_buf, act_buf = rest[-4:]
    t = pl.program_id(1)

    @pl.when(t == 0)
    def _():
        pa_buf[PA_HIST:PA_NEW, :] = sa_ref[0]
        pc_buf[PC_HIST:PC_NEW, :] = sc_ref[0]

    x = x_ref[0]
    hb = _rmsnorm(x, g1_ref[...]).astype(BF16)

    def gate(i):
        lo = COL_GATE + i * D_MODEL
        return jax.nn.sigmoid(_dot(hb, win_ref[:, lo:lo + D_MODEL]))

    zg = _dot(hb, win_ref[:, COL_GLU:COL_GATE])
    pc_buf[PC_NEW:PC_NEW + tl, :] = zg[:, :D_C] * jax.nn.sigmoid(zg[:, D_C:])
    for r in range(tl // CONV_ROWS):
        base = PC_HIST + r * CONV_ROWS
        acc = jnp.broadcast_to(ccb_ref[...], (CONV_ROWS, D_C))
        for off in range(SUBLANES):
            part = None
            for k in range(CONV_C):
                if (base + k) % SUBLANES == off:
                    term = ccw_ref[k:k + 1, :] * pc_buf[base + k:base + k + CONV_ROWS, :]
                    part = term if part is None else part + term
            acc = acc + part
        ln = _layernorm(acc, lcg_ref[...], lcb_ref[...])
        act_buf[r * CONV_ROWS:(r + 1) * CONV_ROWS, :] = (ln * jax.nn.sigmoid(ln)).astype(BF16)
    new_c = pc_buf[PC_HIST + tl:PC_NEW + tl, :]
    nc_ref[0] = new_c
    pc_buf[PC_HIST:PC_NEW, :] = new_c

    za = _dot(hb, win_ref[:, COL_XA:COL_UV])
    pa_buf[PA_NEW:PA_NEW + tl, :] = za[:, 2 * D_A:] * za[:, :D_A]
    conv_a = cab_ref[...]
    for k in range(CONV_A):
        conv_a = conv_a + caw_ref[k:k + 1, :] * pa_buf[PA_HIST + k:PA_HIST + k + tl, :]
    y_a = _dot((za[:, D_A:2 * D_A] * conv_a).astype(BF16), wa_ref[...])
    merged = gate(0) * y_a
    new_a = pa_buf[PA_HIST + tl:PA_NEW + tl, :]
    na_ref[0] = new_a
    pa_buf[PA_HIST:PA_NEW, :] = new_a

    zuv = _dot(hb, win_ref[:, COL_UV:COL_GLU])
    uv = 0.5 * zuv * (1.0 + lax.erf(zuv * (0.5 ** 0.5)))
    v = _layernorm(uv[:, D_B:], lvg_ref[...], lvb_ref[...])
    if emit_v:
        v_ref[0] = v
    vb = v.astype(BF16)
    row = lax.broadcasted_iota(jnp.int32, (cl, cl), 0)
    col = lax.broadcasted_iota(jnp.int32, (cl, cl), 1)
    for g in range(B_GROUPS):
        wg = jnp.where(row >= col, ws_ref[g], 0.0).astype(BF16)
        for c in range(tl // cl):
            blk = vb[c * cl:(c + 1) * cl, g * B_GROUP_DIM:(g + 1) * B_GROUP_DIM]
            s_buf[c * cl:(c + 1) * cl, g * B_GROUP_DIM:(g + 1) * B_GROUP_DIM] = (
                _dot(wg, blk) + bs_ref[:, g * B_GROUP_DIM:(g + 1) * B_GROUP_DIM])
    y_b = _dot((uv[:, :D_B] * s_buf[...]).astype(BF16), wb_ref[...])
    merged = merged + gate(1) * y_b

    y_c = _dot(act_buf[...], wc_ref[...])
    merged = merged + gate(2) * y_c

    xo_ref[0] = x + _dot(merged.astype(BF16), wo_ref[...])


def _token_mix(x, state_a, state_c, p, *, tl, emit_v):
    n, L, _ = x.shape
    cl = min(L, B_CHUNK)
    assert L % tl == 0 and tl % cl == 0 and tl % CONV_ROWS == 0 and tl >= CONV_C - 1
    ws = p['w_s'][:, :cl, :cl]
    bs = jnp.repeat(p['b_s'][:, :cl].T, B_GROUP_DIM, axis=1)
    weights = [p['norm1_g'], p['w_in'], p['conv_a_w'], p['conv_a_b'], p['w_a_out'],
               p['ln_v_g'], p['ln_v_b'], ws, bs, p['w_b_out'], p['conv_c_w'], p['conv_c_b'],
               p['ln_c_g'], p['ln_c_b'], p['w_c_out'], p['w_o']]
    tile = lambda width: pl.BlockSpec((1, tl, width), lambda b, t: (b, t, 0))
    per_seq = lambda rows, width: pl.BlockSpec((1, rows, width), lambda b, t: (b, 0, 0))
    return pl.pallas_call(
        functools.partial(_mix_kernel, tl=tl, cl=cl, emit_v=emit_v),
        grid=(n, L // tl),
        in_specs=[tile(D_MODEL), per_seq(CONV_A - 1, D_A), per_seq(CONV_C - 1, D_C)]
        + [_resident(w.shape) for w in weights],
        out_specs=[tile(D_MODEL), per_seq(CONV_A - 1, D_A), per_seq(CONV_C - 1, D_C)]
        + ([tile(D_B)] if emit_v else []),
        out_shape=[jax.ShapeDtypeStruct((n, L, D_MODEL), F32),
                   jax.ShapeDtypeStruct((n, CONV_A - 1, D_A), F32),
                   jax.ShapeDtypeStruct((n, CONV_C - 1, D_C), F32)]
        + ([jax.ShapeDtypeStruct((n, L, D_B), F32)] if emit_v else []),
        scratch_shapes=[pltpu.VMEM((PA_NEW + tl, D_A), F32),
                        pltpu.VMEM((PC_NEW + tl, D_C), F32),
                        pltpu.VMEM((tl, D_B), F32),
                        pltpu.VMEM((tl, D_C), BF16)],
        compiler_params=pltpu.CompilerParams(
            dimension_semantics=("parallel", "arbitrary"), vmem_limit_bytes=VMEM_LIMIT_BYTES),
        name="token_mix",
    )(x, state_a, state_c, *weights)


def _memkv_kernel(m_ref, g_ref, wk_ref, wv_ref, kf_ref, vf_ref, kh_ref, vh_ref):
    mb = _rmsnorm(m_ref[...], g_ref[0]).astype(BF16)
    for w_ref, flat_ref, heads_ref in ((wk_ref, kf_ref, kh_ref), (wv_ref, vf_ref, vh_ref)):
        proj = _dot(mb, w_ref[0])
        flat_ref[0] = proj
        for h in range(MEM_HEADS):
            heads_ref[0, :, h, :] = proj[:, h * MEM_HEAD_DIM:(h + 1) * MEM_HEAD_DIM]


def _memory_kv(mem, mem_norm_g, w_k, w_v, *, tm):
    depth = w_k.shape[0]
    n, m, _ = mem.shape
    rows = n * m
    assert rows % tm == 0
    flat = pl.BlockSpec((1, tm, D_MODEL), lambda l, i: (l, i, 0))
    heads = pl.BlockSpec((1, tm, MEM_HEADS, MEM_HEAD_DIM), lambda l, i: (l, i, 0, 0))
    weight = pl.BlockSpec((1, D_MODEL, D_MODEL), lambda l, i: (l, 0, 0))
    return pl.pallas_call(
        _memkv_kernel,
        grid=(depth, rows // tm),
        in_specs=[pl.BlockSpec((tm, D_MODEL), lambda l, i: (i, 0)),
                  pl.BlockSpec((1, 1, D_MODEL), lambda l, i: (l, 0, 0)), weight, weight],
        out_specs=[flat, flat, heads, heads],
        out_shape=[jax.ShapeDtypeStruct((depth, rows, D_MODEL), F32)] * 2
        + [jax.ShapeDtypeStruct((depth, rows, MEM_HEADS, MEM_HEAD_DIM), F32)] * 2,
        compiler_params=pltpu.CompilerParams(
            dimension_semantics=("parallel", "parallel"), vmem_limit_bytes=VMEM_LIMIT_BYTES),
        name="memory_kv",
    )(mem.reshape(rows, D_MODEL), mem_norm_g.reshape(depth, 1, D_MODEL), w_k.astype(BF16),
      w_v.astype(BF16))


def _attention(x, k_ref, v_ref, g_ref, wq_ref, wmo_ref, o_buf):
    hb = _rmsnorm(x, g_ref[...]).astype(BF16)
    qb = (_dot(hb, wq_ref[...]) * (MEM_HEAD_DIM ** -0.5)).astype(BF16)
    heads = [slice(h * MEM_HEAD_DIM, (h + 1) * MEM_HEAD_DIM) for h in range(MEM_HEADS)]
    scores = [lax.dot_general(qb[:, cols], k_ref[0, 0, :, cols].astype(BF16),
                              (((1,), (1,)), ((), ())), preferred_element_type=F32)
              for cols in heads]
    for s, cols in zip(scores, heads):
        e = jnp.exp(s - jnp.max(s, axis=-1, keepdims=True))
        prob = e * (1.0 / jnp.sum(e, axis=-1, keepdims=True))
        o_buf[:, cols] = _dot(prob.astype(BF16), v_ref[0, 0, :, cols].astype(BF16)).astype(BF16)
    return x + _dot(o_buf[...], wmo_ref[...])


def _attn_kernel(x_ref, k_ref, v_ref, g_ref, wq_ref, wmo_ref, *rest, route):
    xo_ref, o_buf = rest[-2:]
    xn = _attention(x_ref[0], k_ref, v_ref, g_ref, wq_ref, wmo_ref, o_buf)
    if route:
        g3_ref, wr_ref, br_ref = rest[:3]
        xo_ref[:, :D_MODEL] = xn
        xo_ref[:, D_MODEL:] = _route_info(xn, g3_ref, wr_ref, br_ref)
    else:
        xo_ref[0] = xn


def _cross_attend(x, mem_k, mem_v, layer, p, *, tl, route):
    n, L, _ = x.shape
    assert L % tl == 0
    per_seq_tiles = L // tl
    tile = pl.BlockSpec((1, tl, D_MODEL), lambda b, t: (b, t, 0))
    per_seq = pl.BlockSpec((1, 1, N_MEM, D_MODEL), lambda b, t: (layer, b, 0, 0))
    weights = [p['norm2_g'], p['w_q'], p['w_mo']]
    if route:
        weights += [p['norm3_g'], p['w_router_t'], p['b_router_t']]
        out_spec = pl.BlockSpec((tl, ROUTED_WIDTH), lambda b, t: (b * per_seq_tiles + t, 0))
        out_shape = jax.ShapeDtypeStruct((n * L, ROUTED_WIDTH), F32)
    else:
        out_spec, out_shape = tile, jax.ShapeDtypeStruct((n, L, D_MODEL), F32)
    return pl.pallas_call(
        functools.partial(_attn_kernel, route=route),
        grid=(n, per_seq_tiles),
        in_specs=[tile, per_seq, per_seq] + [_resident(w.shape) for w in weights],
        out_specs=out_spec,
        out_shape=out_shape,
        scratch_shapes=[pltpu.VMEM((tl, D_MODEL), BF16)],
        compiler_params=pltpu.CompilerParams(
            dimension_semantics=("parallel", "arbitrary"), vmem_limit_bytes=VMEM_LIMIT_BYTES),
        name="cross_attend",
    )(x, mem_k, mem_v, *weights)


def _router_logits(x, g_ref, wr_ref, br_ref):
    h = _rmsnorm(x, g_ref[...])
    h_hi = h.astype(BF16)
    h_lo = (h - h_hi.astype(F32)).astype(BF16)
    by_hi = _dot(h_hi, wr_ref[...])
    by_lo = _dot(h_lo, wr_ref[:, :ROUTER_COLS])
    return by_hi[:, :ROUTER_COLS] + (by_hi[:, ROUTER_COLS:] + by_lo) + br_ref[...]


def _route(logits, axis):
    idx = lax.broadcasted_iota(jnp.int32, logits.shape, axis)
    along = lambda f, a: f(a, axis=axis, keepdims=True)
    neg = -jnp.inf
    big = jnp.int32(logits.shape[axis])
    lg = jnp.where(idx < MOE_GROUPS, logits, neg)
    mg = along(jnp.max, lg)
    p_top = 1.0 / along(jnp.sum, jnp.exp(lg - mg))
    g_idx = along(jnp.min, jnp.where(lg == mg, idx, big))
    expert = idx - COL_EXPERT0
    in_group = (expert >= 0) & (expert < N_EXPERTS) & ((expert >> 2) == g_idx)
    le = jnp.where(in_group, logits, neg)
    m1 = along(jnp.max, le)
    i1 = along(jnp.min, jnp.where(le == m1, idx, big))
    le2 = jnp.where(idx == i1, neg, le)
    m2 = along(jnp.max, le2)
    i2 = along(jnp.min, jnp.where(le2 == m2, idx, big))
    e2 = jnp.exp(m2 - m1)
    w1 = p_top / (1.0 + e2)
    w2 = p_top * e2 / (1.0 + e2)
    return idx, g_idx, i1, i2, w1, w2


def _expert_gates(logits):
    col, _, i1, i2, w1, w2 = _route(logits, 1)
    return jnp.where(col == i1, w1, 0.0) + jnp.where(col == i2, w2, 0.0)


def _route_info(x, g_ref, wrt_ref, brt_ref):
    tm = x.shape[0]
    h = _rmsnorm(x, g_ref[...])
    h_hi = h.astype(BF16)
    h_lo = (h - h_hi.astype(F32)).astype(BF16)
    contract_last = (((1,), (1,)), ((), ()))
    by_hi = lax.dot_general(wrt_ref[...], h_hi, contract_last, preferred_element_type=F32)
    by_lo = lax.dot_general(wrt_ref[:ROUTER_COLS, :], h_lo, contract_last,
                            preferred_element_type=F32)
    logits = by_hi[:ROUTER_COLS] + (by_hi[ROUTER_COLS:] + by_lo)
    logits = logits[:ROUTE_ROWS] + brt_ref[...]

    _, g_idx, i1, i2, w1, w2 = _route(logits, 0)
    first_is_lo = i1 < i2
    a = (jnp.minimum(i1, i2) - COL_EXPERT0) & (EXPERTS_PER_GROUP - 1)
    b = (jnp.maximum(i1, i2) - COL_EXPERT0) & (EXPERTS_PER_GROUP - 1)
    cls = g_idx * PAIRS_PER_GROUP + ((a * (2 * EXPERTS_PER_GROUP - 1 - a)) >> 1) + (b - a - 1)
    class_row = lax.broadcasted_iota(jnp.int32, (ROUTE_ROWS, tm), 0)
    onehot = jnp.where(class_row == cls, 1.0, 0.0)
    earlier = lax.broadcasted_iota(jnp.int32, (tm, tm), 0)
    token = lax.broadcasted_iota(jnp.int32, (tm, tm), 1)
    upto = jnp.where(earlier <= token, 1.0, 0.0).astype(BF16)
    running = _dot(onehot.astype(BF16), upto)
    rank = jnp.sum(running * onehot, axis=0, keepdims=True) - 1.0
    w_lo = jnp.where(first_is_lo, w1, w2)
    w_hi = jnp.where(first_is_lo, w2, w1)
    field = lax.broadcasted_iota(jnp.int32, (ROUTER_COLS, tm), 0)
    record = jnp.where(field == INFO_CLASS, cls.astype(F32),
                       jnp.where(field == INFO_RANK, rank,
                                 jnp.where(field == INFO_W_LO, w_lo,
                                           jnp.where(field == INFO_W_HI, w_hi, 0.0))))
    return record.T


def _moe_kernel(x_ref, g_ref, wr_ref, br_ref, w1_ref, w3_ref, w2_ref, gf_ref, o_ref, *, final):
    x = x_ref[...]
    gates = _expert_gates(_router_logits(x, g_ref, wr_ref, br_ref))
    hb = _rmsnorm(x, g_ref[...]).astype(BF16)
    acc = x
    for e in range(N_EXPERTS):
        a = _dot(hb, w1_ref[e])
        hid = a * jax.nn.sigmoid(a) * _dot(hb, w3_ref[e])
        g = gates[:, COL_EXPERT0 + e:COL_EXPERT0 + e + 1]
        acc = acc + _dot((hid * g).astype(BF16), w2_ref[e])
    o_ref[...] = _rmsnorm(acc, gf_ref[...]) if final else acc


def _hier_moe(x, p, final_g, *, tm, final):
    shape = x.shape
    t = x.reshape(-1, D_MODEL)
    rows = t.shape[0]
    assert rows % tm == 0
    tile = pl.BlockSpec((tm, D_MODEL), lambda i: (i, 0))
    weights = [p['norm3_g'], p['w_router'], p['b_router'], p['w1'], p['w3'], p['w2'], final_g]
    out = pl.pallas_call(
        functools.partial(_moe_kernel, final=final),
        grid=(rows // tm,),
        in_specs=[tile] + [_resident(w.shape) for w in weights],
        out_specs=tile,
        out_shape=jax.ShapeDtypeStruct((rows, D_MODEL), F32),
        compiler_params=pltpu.CompilerParams(
            dimension_semantics=("parallel",), vmem_limit_bytes=VMEM_LIMIT_BYTES),
        name="hier_moe",
    )(t, *weights)
    return out.reshape(shape)


def _permute_kernel(pos_ref, src_ref, dst_ref, sem, *, rows, gather):
    def row_copy(j):
        p = pos_ref[0, 0, j]
        if gather:
            return pltpu.make_async_copy(src_ref.at[pl.ds(p, 1)], dst_ref.at[pl.ds(j, 1)], sem)
        return pltpu.make_async_copy(src_ref.at[pl.ds(j, 1)], dst_ref.at[pl.ds(p, 1)], sem)

    for j in range(rows):
        row_copy(j).start(priority=j % DMA_QUEUES)

    def wait(j, carry):
        row_copy(0).wait()
        return carry

    lax.fori_loop(0, rows, wait, 0, unroll=8)


def _permute_rows(src, pos, *, rows, gather):
    n, width = src.shape
    assert n % rows == 0
    steps = n // rows
    tile = pl.BlockSpec((rows, width), lambda s: (s, 0))
    hbm = pl.BlockSpec(memory_space=pl.ANY)
    return pl.pallas_call(
        functools.partial(_permute_kernel, rows=rows, gather=gather),
        grid=(steps,),
        in_specs=[pl.BlockSpec((1, 1, rows), lambda s: (s, 0, 0), memory_space=pltpu.SMEM),
                  hbm if gather else tile],
        out_specs=tile if gather else hbm,
        out_shape=jax.ShapeDtypeStruct((n, width), src.dtype),
        scratch_shapes=[pltpu.SemaphoreType.DMA(())],
        compiler_params=pltpu.CompilerParams(dimension_semantics=("arbitrary",)),
        name="gather_rows" if gather else "scatter_rows",
    )(pos.reshape(steps, 1, rows), src)


def _routing_plan(cls, rank, *, rank_tile, tm):
    n = cls.shape[0]
    tiles = n // tm
    rank_tiles = n // rank_tile
    onehot = cls.reshape(rank_tiles, rank_tile, 1) == jnp.arange(N_CLASSES, dtype=jnp.int32)
    counts = jnp.sum(onehot.astype(jnp.int32), axis=1)
    class_end = jnp.cumsum(jnp.sum(counts, axis=0))
    class_start = class_end - jnp.sum(counts, axis=0)
    tile_base = class_start[None, :] + jnp.cumsum(counts, axis=0) - counts
    pos = jnp.sum(jnp.where(onehot, tile_base[:, None, :], 0), axis=2).reshape(n) + rank
    tile_cut = jnp.arange(tiles, dtype=jnp.int32) * tm
    class_cut = class_start[1:]
    tile_slot = jnp.arange(tiles) + jnp.sum(class_cut[None, :] < tile_cut[:, None], axis=1)
    class_slot = (jnp.arange(N_CLASSES - 1)
                  + jnp.sum(tile_cut[None, :] <= class_cut[:, None], axis=1))
    slots = jnp.arange(tiles + N_CLASSES - 1)[:, None]
    lo = (jnp.sum(jnp.where(tile_slot[None, :] == slots, tile_cut[None, :], 0), axis=1)
          + jnp.sum(jnp.where(class_slot[None, :] == slots, class_cut[None, :], 0), axis=1))
    hi = jnp.concatenate([lo[1:], jnp.array([n], lo.dtype)])
    tile = jnp.minimum(lo // tm, tiles - 1)
    c = jnp.minimum(jnp.sum(class_end[None, :] <= lo[:, None], axis=1), N_CLASSES - 1)
    first = ((hi > lo) & (lo == tile * tm)).astype(jnp.int32)
    group, pair = c // PAIRS_PER_GROUP, c % PAIRS_PER_GROUP
    a = (pair >= EXPERTS_PER_GROUP - 1).astype(jnp.int32) + (pair >= 2 * EXPERTS_PER_GROUP - 3)
    b = pair - ((a * (2 * EXPERTS_PER_GROUP - 1 - a)) >> 1) + a + 1
    e_lo = group * EXPERTS_PER_GROUP + a
    e_hi = group * EXPERTS_PER_GROUP + b
    as_i32 = lambda a: a.astype(jnp.int32)
    return as_i32(pos), tuple(as_i32(a) for a in (tile, e_lo, e_hi, lo, hi, first))


def _moe_visit_kernel(tile_ref, elo_ref, ehi_ref, lo_ref, hi_ref, first_ref, xr_ref, g_ref,
                      w1a_ref, w1b_ref, w3a_ref, w3b_ref, w2a_ref, w2b_ref, gf_ref, o_ref, *,
                      tm, final):
    v = pl.program_id(0)
    lo, hi = lo_ref[v], hi_ref[v]

    @pl.when(hi > lo)
    def _():
        x = xr_ref[:, :D_MODEL]
        info = xr_ref[:, D_MODEL:]
        hb = _rmsnorm(x, g_ref[...]).astype(BF16)

        def hidden(w1_ref, w3_ref, gate):
            a = _dot(hb, w1_ref[0])
            return (a * jax.nn.sigmoid(a) * _dot(hb, w3_ref[0]) * gate).astype(BF16)

        h_lo = hidden(w1a_ref, w3a_ref, info[:, INFO_W_LO:INFO_W_LO + 1])
        h_hi = hidden(w1b_ref, w3b_ref, info[:, INFO_W_HI:INFO_W_HI + 1])
        res = x + (_dot(h_lo, w2a_ref[0]) + _dot(h_hi, w2b_ref[0]))
        if final:
            res = _rmsnorm(res, gf_ref[...])
        row = tile_ref[v] * tm + lax.broadcasted_iota(jnp.int32, (tm, 1), 0)
        mine = (row >= lo) & (row < hi)

        @pl.when(first_ref[v] == 1)
        def _():
            o_ref[...] = jnp.where(mine, res, 0.0)

        @pl.when(first_ref[v] == 0)
        def _():
            o_ref[...] = jnp.where(mine, res, o_ref[...])


def _moe_visits(xr_sorted, visits, p, final_g, *, tm, final):
    n = xr_sorted.shape[0]
    n_visits = visits[0].shape[0]
    by_tile = lambda width: pl.BlockSpec((tm, width), lambda v, tile, *_: (tile[v], 0))
    expert_lo = lambda shape: pl.BlockSpec((1,) + shape, lambda v, tile, elo, *_: (elo[v], 0, 0))
    expert_hi = lambda shape: pl.BlockSpec((1,) + shape,
                                           lambda v, tile, elo, ehi, *_: (ehi[v], 0, 0))
    up, down = (D_MODEL, D_EXPERT), (D_EXPERT, D_MODEL)
    return pl.pallas_call(
        functools.partial(_moe_visit_kernel, tm=tm, final=final),
        grid_spec=pltpu.PrefetchScalarGridSpec(
            num_scalar_prefetch=len(visits),
            grid=(n_visits,),
            in_specs=[by_tile(ROUTED_WIDTH), _resident(p['norm3_g'].shape),
                      expert_lo(up), expert_hi(up), expert_lo(up), expert_hi(up),
                      expert_lo(down), expert_hi(down), _resident(final_g.shape)],
            out_specs=by_tile(D_MODEL)),
        out_shape=jax.ShapeDtypeStruct((n, D_MODEL), F32),
        compiler_params=pltpu.CompilerParams(
            dimension_semantics=("arbitrary",), vmem_limit_bytes=VMEM_LIMIT_BYTES),
        name="moe_visits",
    )(*visits, xr_sorted, p['norm3_g'], p['w1'], p['w1'], p['w3'], p['w3'], p['w2'], p['w2'],
      final_g)


def _routed_moe(flat, p, final_g, *, rank_tile, tm, final):
    cls = flat[:, D_MODEL + INFO_CLASS].astype(jnp.int32)
    rank = flat[:, D_MODEL + INFO_RANK].astype(jnp.int32)
    pos, visits = _routing_plan(cls, rank, rank_tile=rank_tile, tm=tm)
    xr_sorted = _permute_rows(flat, pos, rows=PERMUTE_ROWS, gather=False)
    y_sorted = _moe_visits(xr_sorted, visits, p, final_g, tm=tm, final=final)
    return _permute_rows(y_sorted, pos, rows=PERMUTE_ROWS, gather=True)


def _split_bf16(w):
    hi = w.astype(BF16)
    lo = (w - hi.astype(F32)).astype(BF16)
    return jnp.concatenate([hi, lo], axis=1)


def _layer_params(l, norm1_g, w_in, conv_a_w, conv_a_b, w_a_out, ln_v_g, ln_v_b, w_s, b_s,
                  w_b_out, conv_c_w, conv_c_b, ln_c_g, ln_c_b, w_c_out, w_o, norm2_g,
                  w_q, w_mo, norm3_g, rg_w, rg_b, re_w, re_b, w1, w3, w2):
    row = lambda a: a[l].reshape(1, -1)
    bf = lambda a: a[l].astype(BF16)
    w_router = jnp.concatenate([rg_w[l], re_w[l].reshape(D_MODEL, N_EXPERTS)], axis=1)
    b_router = jnp.concatenate([rg_b[l], re_b[l].reshape(N_EXPERTS)])
    pad = ROUTER_COLS - MOE_GROUPS - N_EXPERTS
    return dict(
        norm1_g=row(norm1_g), w_in=bf(w_in), conv_a_w=conv_a_w[l], conv_a_b=row(conv_a_b),
        w_a_out=bf(w_a_out), ln_v_g=row(ln_v_g), ln_v_b=row(ln_v_b), w_s=w_s[l], b_s=b_s[l],
        w_b_out=bf(w_b_out), conv_c_w=conv_c_w[l], conv_c_b=row(conv_c_b), ln_c_g=row(ln_c_g),
        ln_c_b=row(ln_c_b), w_c_out=bf(w_c_out), w_o=bf(w_o), norm2_g=row(norm2_g),
        w_q=bf(w_q), w_mo=bf(w_mo),
        norm3_g=row(norm3_g), w_router=_split_bf16(jnp.pad(w_router, ((0, 0), (0, pad)))),
        b_router=jnp.pad(b_router, (0, pad)).reshape(1, -1),
        w_router_t=_split_bf16(jnp.pad(w_router, ((0, 0), (0, pad)))).T,
        b_router_t=jnp.pad(b_router, (0, ROUTE_ROWS - MOE_GROUPS - N_EXPERTS)).reshape(-1, 1),
        w1=bf(w1), w3=bf(w3), w2=bf(w2))


def kernel(x_prompt, x_sample, state_conv_a, state_conv_c, cache_mem_k, cache_mem_v, mem_prompt, norm1_g, w_in, conv_a_w, conv_a_b, w_a_out, ln_v_g, ln_v_b, w_s, b_s, w_b_out, conv_c_w, conv_c_b, ln_c_g, ln_c_b, w_c_out, w_o, norm2_g, mem_norm_g, w_q, w_k, w_v, w_mo, norm3_g, rg_w, rg_b, re_w, re_b, w1, w3, w2, final_norm_g):
    depth = w_in.shape[0]
    n_p, n_s = x_prompt.shape[0], x_sample.shape[0]
    final_g = final_norm_g.reshape(1, -1)
    x_p, x_s = x_prompt, x_sample
    zero_a = jnp.zeros((n_p, CONV_A - 1, D_A), F32)
    zero_c = jnp.zeros((n_p, CONV_C - 1, D_C), F32)
    sa_p, sa_s, sc_p, sc_s, v_s = [], [], [], [], []
    mk, mv, mk_heads, mv_heads = _memory_kv(mem_prompt, mem_norm_g, w_k, w_v, tm=512)
    mk = mk.reshape(depth, n_p, N_MEM, D_MODEL)
    mv = mv.reshape(depth, n_p, N_MEM, D_MODEL)
    ck = cache_mem_k.reshape(depth, n_s, N_MEM, D_MODEL)
    cv = cache_mem_v.reshape(depth, n_s, N_MEM, D_MODEL)
    for l in range(depth):
        p = _layer_params(l, norm1_g, w_in, conv_a_w, conv_a_b, w_a_out, ln_v_g, ln_v_b, w_s,
                          b_s, w_b_out, conv_c_w, conv_c_b, ln_c_g, ln_c_b, w_c_out, w_o,
                          norm2_g, w_q, w_mo, norm3_g, rg_w, rg_b, re_w,
                          re_b, w1, w3, w2)
        final = l == depth - 1
        x_p, na, nc = _token_mix(x_p, zero_a, zero_c, p, tl=512, emit_v=False)
        xr_p = _cross_attend(x_p, mk, mv, l, p, tl=512, route=True)
        x_p = _routed_moe(xr_p, p, final_g, rank_tile=512, tm=512, final=final).reshape(x_p.shape)
        sa_p.append(na)
        sc_p.append(nc)
        x_s, na, nc, vr = _token_mix(x_s, state_conv_a[l], state_conv_c[l], p, tl=x_s.shape[1],
                                     emit_v=True)
        x_s = _cross_attend(x_s, ck, cv, l, p, tl=x_s.shape[1], route=False)
        x_s = _hier_moe(x_s, p, final_g, tm=x_s.shape[0] * x_s.shape[1], final=final)
        sa_s.append(na)
        sc_s.append(nc)
        v_s.append(vr)
    return (x_p, x_s, jnp.stack(sa_p), jnp.stack(sa_s), jnp.stack(sc_p), jnp.stack(sc_s),
            jnp.stack(v_s), mk_heads.reshape(depth, n_p, N_MEM, MEM_HEADS, MEM_HEAD_DIM),
            mv_heads.reshape(depth, n_p, N_MEM, MEM_HEADS, MEM_HEAD_DIM))
```

```python
import functools

import jax
import jax.numpy as jnp
from jax import lax
from jax.experimental import pallas as pl
from jax.experimental.pallas import tpu as pltpu

D_MODEL = 1024
D_A = D_MODEL // 2
CONV_A = 3
D_B = D_MODEL // 2
B_GROUPS = 4
B_GROUP_DIM = D_B // B_GROUPS
B_CHUNK = 128
D_C = D_MODEL // 2
CONV_C = 31
N_MEM = 256
MEM_HEADS = 4
MEM_HEAD_DIM = D_MODEL // MEM_HEADS
MOE_GROUPS = 4
EXPERTS_PER_GROUP = 4
N_EXPERTS = MOE_GROUPS * EXPERTS_PER_GROUP
D_EXPERT = D_MODEL // 4
EPS = 1e-6
COL_XA = 0
COL_BA = COL_XA + D_A
COL_CA = COL_BA + D_A
COL_UV = COL_CA + D_A
COL_GLU = COL_UV + 2 * D_B
COL_GATE = COL_GLU + 2 * D_C
D_IN = COL_GATE + 3 * D_MODEL

SUBLANES = 8
LANES = 128
VMEM_LIMIT_BYTES = 56 * 1024 * 1024
DMA_QUEUES = 2

PA_NEW = SUBLANES
PA_HIST = PA_NEW - (CONV_A - 1)
PC_NEW = 32
PC_HIST = PC_NEW - (CONV_C - 1)
CONV_ROWS = 64
ROUTER_COLS = LANES
COL_EXPERT0 = MOE_GROUPS
ROUTE_ROWS = 32
PAIRS_PER_GROUP = EXPERTS_PER_GROUP * (EXPERTS_PER_GROUP - 1) // 2
N_CLASSES = MOE_GROUPS * PAIRS_PER_GROUP
INFO_CLASS, INFO_RANK, INFO_W_LO, INFO_W_HI = 0, 1, 2, 3
ROUTED_WIDTH = D_MODEL + ROUTER_COLS
PERMUTE_ROWS = 1024

BF16 = jnp.bfloat16
F32 = jnp.float32


def _dot(a, b):
    return jnp.dot(a, b, preferred_element_type=F32)


def _rmsnorm(x, g):
    ms = jnp.mean(x * x, axis=-1, keepdims=True)
    return x * lax.rsqrt(ms + EPS) * g


def _layernorm(x, g, b):
    xc = x - jnp.mean(x, axis=-1, keepdims=True)
    var = jnp.mean(xc * xc, axis=-1, keepdims=True)
    return xc * lax.rsqrt(var + EPS) * g + b


def _sigmoid(x):
    return 0.5 * jnp.tanh(0.5 * x) + 0.5


def _resident(shape):
    nd = len(shape)
    return pl.BlockSpec(shape, lambda *_: (0,) * nd, pipeline_mode=pl.Buffered(1))


def _mix_kernel(x_ref, sa_ref, sc_ref, g1_ref, win_ref, caw_ref, cab_ref, wa_ref, lvg_ref,
                lvb_ref, ws_ref, bs_ref, wb_ref, ccw_ref, ccb_ref, lcg_ref, lcb_ref, wc_ref,
                wo_ref, xo_ref, na_ref, nc_ref, *rest, tl, cl, emit_v):
    v_ref = rest[0] if emit_v else None
    pa_buf, pc_buf, s_buf, act_buf = rest[-4:]
    t = pl.program_id(1)

    @pl.when(t == 0)
    def _():
        pa_buf[PA_HIST:PA_NEW, :] = sa_ref[0]
        pc_buf[PC_HIST:PC_NEW, :] = sc_ref[0]

    x = x_ref[0]
    hb = _rmsnorm(x, g1_ref[...]).astype(BF16)

    def gate(i):
        lo = COL_GATE + i * D_MODEL
        return _sigmoid(_dot(hb, win_ref[:, lo:lo + D_MODEL]))

    zg = _dot(hb, win_ref[:, COL_GLU:COL_GATE])
    pc_buf[PC_NEW:PC_NEW + tl, :] = zg[:, :D_C] * _sigmoid(zg[:, D_C:])
    for r in range(tl // CONV_ROWS):
        base = PC_HIST + r * CONV_ROWS
        acc = jnp.broadcast_to(ccb_ref[...], (CONV_ROWS, D_C))
        for off in range(SUBLANES):
            part = None
            for k in range(CONV_C):
                if (base + k) % SUBLANES == off:
                    term = ccw_ref[k:k + 1, :] * pc_buf[base + k:base + k + CONV_ROWS, :]
                    part = term if part is None else part + term
            acc = acc + part
        ln = _layernorm(acc, lcg_ref[...], lcb_ref[...])
        act_buf[r * CONV_ROWS:(r + 1) * CONV_ROWS, :] = (ln * _sigmoid(ln)).astype(BF16)
    new_c = pc_buf[PC_HIST + tl:PC_NEW + tl, :]
    nc_ref[0] = new_c
    pc_buf[PC_HIST:PC_NEW, :] = new_c

    za = _dot(hb, win_ref[:, COL_XA:COL_UV])
    pa_buf[PA_NEW:PA_NEW + tl, :] = za[:, 2 * D_A:] * za[:, :D_A]
    conv_a = cab_ref[...]
    for k in range(CONV_A):
        conv_a = conv_a + caw_ref[k:k + 1, :] * pa_buf[PA_HIST + k:PA_HIST + k + tl, :]
    gates = [gate(i) for i in range(3)]
    y_a = _dot((za[:, D_A:2 * D_A] * conv_a).astype(BF16), wa_ref[...])
    merged = gates[0] * y_a
    new_a = pa_buf[PA_HIST + tl:PA_NEW + tl, :]
    na_ref[0] = new_a
    pa_buf[PA_HIST:PA_NEW, :] = new_a

    zuv = _dot(hb, win_ref[:, COL_UV:COL_GLU])
    uv = 0.5 * zuv * (1.0 + lax.erf(zuv * (0.5 ** 0.5)))
    v = _layernorm(uv[:, D_B:], lvg_ref[...], lvb_ref[...])
    if emit_v:
        v_ref[0] = v
    vb = v.astype(BF16)
    row = lax.broadcasted_iota(jnp.int32, (cl, cl), 0)
    col = lax.broadcasted_iota(jnp.int32, (cl, cl), 1)
    for g in range(B_GROUPS):
        wg = jnp.where(row >= col, ws_ref[g], 0.0).astype(BF16)
        for c in range(tl // cl):
            blk = vb[c * cl:(c + 1) * cl, g * B_GROUP_DIM:(g + 1) * B_GROUP_DIM]
            s_buf[c * cl:(c + 1) * cl, g * B_GROUP_DIM:(g + 1) * B_GROUP_DIM] = (
                _dot(wg, blk) + bs_ref[:, g * B_GROUP_DIM:(g + 1) * B_GROUP_DIM])
    y_b = _dot((uv[:, :D_B] * s_buf[...]).astype(BF16), wb_ref[...])
    merged = merged + gates[1] * y_b

    y_c = _dot(act_buf[...], wc_ref[...])
    merged = merged + gates[2] * y_c

    xo_ref[0] = x + _dot(merged.astype(BF16), wo_ref[...])


def _token_mix(x, state_a, state_c, p, *, tl, emit_v):
    n, L, _ = x.shape
    cl = min(L, B_CHUNK)
    assert L % tl == 0 and tl % cl == 0 and tl % CONV_ROWS == 0 and tl >= CONV_C - 1
    ws = p['w_s'][:, :cl, :cl]
    bs = jnp.repeat(p['b_s'][:, :cl].T, B_GROUP_DIM, axis=1)
    weights = [p['norm1_g'], p['w_in'], p['conv_a_w'], p['conv_a_b'], p['w_a_out'],
               p['ln_v_g'], p['ln_v_b'], ws, bs, p['w_b_out'], p['conv_c_w'], p['conv_c_b'],
               p['ln_c_g'], p['ln_c_b'], p['w_c_out'], p['w_o']]
    tile = lambda width: pl.BlockSpec((1, tl, width), lambda b, t: (b, t, 0))
    per_seq = lambda rows, width: pl.BlockSpec((1, rows, width), lambda b, t: (b, 0, 0))
    return pl.pallas_call(
        functools.partial(_mix_kernel, tl=tl, cl=cl, emit_v=emit_v),
        grid=(n, L // tl),
        in_specs=[tile(D_MODEL), per_seq(CONV_A - 1, D_A), per_seq(CONV_C - 1, D_C)]
        + [_resident(w.shape) for w in weights],
        out_specs=[tile(D_MODEL), per_seq(CONV_A - 1, D_A), per_seq(CONV_C - 1, D_C)]
        + ([tile(D_B)] if emit_v else []),
        out_shape=[jax.ShapeDtypeStruct((n, L, D_MODEL), F32),
                   jax.ShapeDtypeStruct((n, CONV_A - 1, D_A), F32),
                   jax.ShapeDtypeStruct((n, CONV_C - 1, D_C), F32)]
        + ([jax.ShapeDtypeStruct((n, L, D_B), F32)] if emit_v else []),
        scratch_shapes=[pltpu.VMEM((PA_NEW + tl, D_A), F32),
                        pltpu.VMEM((PC_NEW + tl, D_C), F32),
                        pltpu.VMEM((tl, D_B), F32),
                        pltpu.VMEM((tl, D_C), BF16)],
        compiler_params=pltpu.CompilerParams(
            dimension_semantics=("parallel", "arbitrary"), vmem_limit_bytes=VMEM_LIMIT_BYTES),
        name="token_mix",
    )(x, state_a, state_c, *weights)


def _memkv_kernel(m_ref, g_ref, wk_ref, wv_ref, kf_ref, vf_ref, kh_ref, vh_ref):
    mb = _rmsnorm(m_ref[...], g_ref[0]).astype(BF16)
    for w_ref, flat_ref, heads_ref in ((wk_ref, kf_ref, kh_ref), (wv_ref, vf_ref, vh_ref)):
        proj = _dot(mb, w_ref[0])
        flat_ref[0] = proj
        for h in range(MEM_HEADS):
            heads_ref[0, :, h, :] = proj[:, h * MEM_HEAD_DIM:(h + 1) * MEM_HEAD_DIM]


def _memory_kv(mem, mem_norm_g, w_k, w_v, *, tm):
    depth = w_k.shape[0]
    n, m, _ = mem.shape
    rows = n * m
    assert rows % tm == 0
    flat = pl.BlockSpec((1, tm, D_MODEL), lambda l, i: (l, i, 0))
    heads = pl.BlockSpec((1, tm, MEM_HEADS, MEM_HEAD_DIM), lambda l, i: (l, i, 0, 0))
    weight = pl.BlockSpec((1, D_MODEL, D_MODEL), lambda l, i: (l, 0, 0))
    return pl.pallas_call(
        _memkv_kernel,
        grid=(depth, rows // tm),
        in_specs=[pl.BlockSpec((tm, D_MODEL), lambda l, i: (i, 0)),
                  pl.BlockSpec((1, 1, D_MODEL), lambda l, i: (l, 0, 0)), weight, weight],
        out_specs=[flat, flat, heads, heads],
        out_shape=[jax.ShapeDtypeStruct((depth, rows, D_MODEL), F32)] * 2
        + [jax.ShapeDtypeStruct((depth, rows, MEM_HEADS, MEM_HEAD_DIM), F32)] * 2,
        compiler_params=pltpu.CompilerParams(
            dimension_semantics=("parallel", "parallel"), vmem_limit_bytes=VMEM_LIMIT_BYTES),
        name="memory_kv",
    )(mem.reshape(rows, D_MODEL), mem_norm_g.reshape(depth, 1, D_MODEL), w_k.astype(BF16),
      w_v.astype(BF16))


def _attention(x, k_ref, v_ref, g_ref, wq_ref, wmo_ref, o_buf):
    hb = _rmsnorm(x, g_ref[...]).astype(BF16)
    qb = (_dot(hb, wq_ref[...]) * (MEM_HEAD_DIM ** -0.5)).astype(BF16)
    heads = [slice(h * MEM_HEAD_DIM, (h + 1) * MEM_HEAD_DIM) for h in range(MEM_HEADS)]
    scores = [lax.dot_general(qb[:, cols], k_ref[0, 0, :, cols].astype(BF16),
                              (((1,), (1,)), ((), ())), preferred_element_type=F32)
              for cols in heads]
    for s, cols in zip(scores, heads):
        e = jnp.exp(s - jnp.max(s, axis=-1, keepdims=True))
        prob = e * (1.0 / jnp.sum(e, axis=-1, keepdims=True))
        o_buf[:, cols] = _dot(prob.astype(BF16), v_ref[0, 0, :, cols].astype(BF16)).astype(BF16)
    return x + _dot(o_buf[...], wmo_ref[...])


def _attn_kernel(x_ref, k_ref, v_ref, g_ref, wq_ref, wmo_ref, *rest, route):
    xo_ref, o_buf = rest[-2:]
    xn = _attention(x_ref[0], k_ref, v_ref, g_ref, wq_ref, wmo_ref, o_buf)
    if route:
        g3_ref, wr_ref, br_ref = rest[:3]
        xo_ref[:, :D_MODEL] = xn
        xo_ref[:, D_MODEL:] = _route_info(xn, g3_ref, wr_ref, br_ref)
    else:
        xo_ref[0] = xn


def _cross_attend(x, mem_k, mem_v, layer, p, *, tl, route):
    n, L, _ = x.shape
    assert L % tl == 0
    per_seq_tiles = L // tl
    tile = pl.BlockSpec((1, tl, D_MODEL), lambda b, t: (b, t, 0))
    per_seq = pl.BlockSpec((1, 1, N_MEM, D_MODEL), lambda b, t: (layer, b, 0, 0))
    weights = [p['norm2_g'], p['w_q'], p['w_mo']]
    if route:
        weights += [p['norm3_g'], p['w_router_t'], p['b_router_t']]
        out_spec = pl.BlockSpec((tl, ROUTED_WIDTH), lambda b, t: (b * per_seq_tiles + t, 0))
        out_shape = jax.ShapeDtypeStruct((n * L, ROUTED_WIDTH), F32)
    else:
        out_spec, out_shape = tile, jax.ShapeDtypeStruct((n, L, D_MODEL), F32)
    return pl.pallas_call(
        functools.partial(_attn_kernel, route=route),
        grid=(n, per_seq_tiles),
        in_specs=[tile, per_seq, per_seq] + [_resident(w.shape) for w in weights],
        out_specs=out_spec,
        out_shape=out_shape,
        scratch_shapes=[pltpu.VMEM((tl, D_MODEL), BF16)],
        compiler_params=pltpu.CompilerParams(
            dimension_semantics=("parallel", "arbitrary"), vmem_limit_bytes=VMEM_LIMIT_BYTES),
        name="cross_attend",
    )(x, mem_k, mem_v, *weights)


def _router_logits(x, g_ref, wr_ref, br_ref):
    h = _rmsnorm(x, g_ref[...])
    h_hi = h.astype(BF16)
    h_lo = (h - h_hi.astype(F32)).astype(BF16)
    by_hi = _dot(h_hi, wr_ref[...])
    by_lo = _dot(h_lo, wr_ref[:, :ROUTER_COLS])
    return by_hi[:, :ROUTER_COLS] + (by_hi[:, ROUTER_COLS:] + by_lo) + br_ref[...]


def _route(logits, axis):
    idx = lax.broadcasted_iota(jnp.int32, logits.shape, axis)
    along = lambda f, a: f(a, axis=axis, keepdims=True)
    neg = -jnp.inf
    big = jnp.int32(logits.shape[axis])
    lg = jnp.where(idx < MOE_GROUPS, logits, neg)
    mg = along(jnp.max, lg)
    p_top = 1.0 / along(jnp.sum, jnp.exp(lg - mg))
    g_idx = along(jnp.min, jnp.where(lg == mg, idx, big))
    expert = idx - COL_EXPERT0
    in_group = (expert >= 0) & (expert < N_EXPERTS) & ((expert >> 2) == g_idx)
    le = jnp.where(in_group, logits, neg)
    m1 = along(jnp.max, le)
    i1 = along(jnp.min, jnp.where(le == m1, idx, big))
    le2 = jnp.where(idx == i1, neg, le)
    m2 = along(jnp.max, le2)
    i2 = along(jnp.min, jnp.where(le2 == m2, idx, big))
    e2 = jnp.exp(m2 - m1)
    w1 = p_top / (1.0 + e2)
    w2 = p_top * e2 / (1.0 + e2)
    return idx, g_idx, i1, i2, w1, w2


def _expert_gates(logits):
    col, _, i1, i2, w1, w2 = _route(logits, 1)
    return jnp.where(col == i1, w1, 0.0) + jnp.where(col == i2, w2, 0.0)


def _route_info(x, g_ref, wrt_ref, brt_ref):
    tm = x.shape[0]
    h = _rmsnorm(x, g_ref[...])
    h_hi = h.astype(BF16)
    h_lo = (h - h_hi.astype(F32)).astype(BF16)
    contract_last = (((1,), (1,)), ((), ()))
    by_hi = lax.dot_general(wrt_ref[...], h_hi, contract_last, preferred_element_type=F32)
    by_lo = lax.dot_general(wrt_ref[:ROUTER_COLS, :], h_lo, contract_last,
                            preferred_element_type=F32)
    logits = by_hi[:ROUTER_COLS] + (by_hi[ROUTER_COLS:] + by_lo)
    logits = logits[:ROUTE_ROWS] + brt_ref[...]

    _, g_idx, i1, i2, w1, w2 = _route(logits, 0)
    first_is_lo = i1 < i2
    a = (jnp.minimum(i1, i2) - COL_EXPERT0) & (EXPERTS_PER_GROUP - 1)
    b = (jnp.maximum(i1, i2) - COL_EXPERT0) & (EXPERTS_PER_GROUP - 1)
    cls = g_idx * PAIRS_PER_GROUP + ((a * (2 * EXPERTS_PER_GROUP - 1 - a)) >> 1) + (b - a - 1)
    class_row = lax.broadcasted_iota(jnp.int32, (ROUTE_ROWS, tm), 0)
    onehot = jnp.where(class_row == cls, 1.0, 0.0)
    earlier = lax.broadcasted_iota(jnp.int32, (tm, tm), 0)
    token = lax.broadcasted_iota(jnp.int32, (tm, tm), 1)
    upto = jnp.where(earlier <= token, 1.0, 0.0).astype(BF16)
    running = _dot(onehot.astype(BF16), upto)
    rank = jnp.sum(running * onehot, axis=0, keepdims=True) - 1.0
    w_lo = jnp.where(first_is_lo, w1, w2)
    w_hi = jnp.where(first_is_lo, w2, w1)
    field = lax.broadcasted_iota(jnp.int32, (ROUTER_COLS, tm), 0)
    record = jnp.where(field == INFO_CLASS, cls.astype(F32),
                       jnp.where(field == INFO_RANK, rank,
                                 jnp.where(field == INFO_W_LO, w_lo,
                                           jnp.where(field == INFO_W_HI, w_hi, 0.0))))
    return record.T


def _moe_kernel(x_ref, g_ref, wr_ref, br_ref, w1_ref, w3_ref, w2_ref, gf_ref, o_ref, *, final):
    x = x_ref[...]
    gates = _expert_gates(_router_logits(x, g_ref, wr_ref, br_ref))
    hb = _rmsnorm(x, g_ref[...]).astype(BF16)
    acc = x
    for e in range(N_EXPERTS):
        a = _dot(hb, w1_ref[e])
        hid = a * _sigmoid(a) * _dot(hb, w3_ref[e])
        g = gates[:, COL_EXPERT0 + e:COL_EXPERT0 + e + 1]
        acc = acc + _dot((hid * g).astype(BF16), w2_ref[e])
    o_ref[...] = _rmsnorm(acc, gf_ref[...]) if final else acc


def _hier_moe(x, p, final_g, *, tm, final):
    shape = x.shape
    t = x.reshape(-1, D_MODEL)
    rows = t.shape[0]
    assert rows % tm == 0
    tile = pl.BlockSpec((tm, D_MODEL), lambda i: (i, 0))
    weights = [p['norm3_g'], p['w_router'], p['b_router'], p['w1'], p['w3'], p['w2'], final_g]
    out = pl.pallas_call(
        functools.partial(_moe_kernel, final=final),
        grid=(rows // tm,),
        in_specs=[tile] + [_resident(w.shape) for w in weights],
        out_specs=tile,
        out_shape=jax.ShapeDtypeStruct((rows, D_MODEL), F32),
        compiler_params=pltpu.CompilerParams(
            dimension_semantics=("parallel",), vmem_limit_bytes=VMEM_LIMIT_BYTES),
        name="hier_moe",
    )(t, *weights)
    return out.reshape(shape)


def _permute_kernel(pos_ref, src_ref, dst_ref, sem, *, rows, gather):
    def row_copy(j):
        p = pos_ref[0, 0, j]
        if gather:
            return pltpu.make_async_copy(src_ref.at[pl.ds(p, 1)], dst_ref.at[pl.ds(j, 1)], sem)
        return pltpu.make_async_copy(src_ref.at[pl.ds(j, 1)], dst_ref.at[pl.ds(p, 1)], sem)

    for j in range(rows):
        row_copy(j).start(priority=j % DMA_QUEUES)

    def wait(j, carry):
        row_copy(0).wait()
        return carry

    lax.fori_loop(0, rows, wait, 0, unroll=8)


def _permute_rows(src, pos, *, rows, gather):
    n, width = src.shape
    assert n % rows == 0
    steps = n // rows
    tile = pl.BlockSpec((rows, width), lambda s: (s, 0))
    hbm = pl.BlockSpec(memory_space=pl.ANY)
    return pl.pallas_call(
        functools.partial(_permute_kernel, rows=rows, gather=gather),
        grid=(steps,),
        in_specs=[pl.BlockSpec((1, 1, rows), lambda s: (s, 0, 0), memory_space=pltpu.SMEM),
                  hbm if gather else tile],
        out_specs=tile if gather else hbm,
        out_shape=jax.ShapeDtypeStruct((n, width), src.dtype),
        scratch_shapes=[pltpu.SemaphoreType.DMA(())],
        compiler_params=pltpu.CompilerParams(dimension_semantics=("arbitrary",)),
        name="gather_rows" if gather else "scatter_rows",
    )(pos.reshape(steps, 1, rows), src)


def _routing_plan(cls, rank, *, rank_tile, tm):
    n = cls.shape[0]
    tiles = n // tm
    rank_tiles = n // rank_tile
    onehot = cls.reshape(rank_tiles, rank_tile, 1) == jnp.arange(N_CLASSES, dtype=jnp.int32)
    counts = jnp.sum(onehot.astype(jnp.int32), axis=1)
    class_end = jnp.cumsum(jnp.sum(counts, axis=0))
    class_start = class_end - jnp.sum(counts, axis=0)
    tile_base = class_start[None, :] + jnp.cumsum(counts, axis=0) - counts
    pos = jnp.sum(jnp.where(onehot, tile_base[:, None, :], 0), axis=2).reshape(n) + rank
    tile_cut = jnp.arange(tiles, dtype=jnp.int32) * tm
    class_cut = class_start[1:]
    tile_slot = jnp.arange(tiles) + jnp.sum(class_cut[None, :] < tile_cut[:, None], axis=1)
    class_slot = (jnp.arange(N_CLASSES - 1)
                  + jnp.sum(tile_cut[None, :] <= class_cut[:, None], axis=1))
    slots = jnp.arange(tiles + N_CLASSES - 1)[:, None]
    lo = (jnp.sum(jnp.where(tile_slot[None, :] == slots, tile_cut[None, :], 0), axis=1)
          + jnp.sum(jnp.where(class_slot[None, :] == slots, class_cut[None, :], 0), axis=1))
    hi = jnp.concatenate([lo[1:], jnp.array([n], lo.dtype)])
    tile = jnp.minimum(lo // tm, tiles - 1)
    c = jnp.minimum(jnp.sum(class_end[None, :] <= lo[:, None], axis=1), N_CLASSES - 1)
    first = ((hi > lo) & (lo == tile * tm)).astype(jnp.int32)
    group, pair = c // PAIRS_PER_GROUP, c % PAIRS_PER_GROUP
    a = (pair >= EXPERTS_PER_GROUP - 1).astype(jnp.int32) + (pair >= 2 * EXPERTS_PER_GROUP - 3)
    b = pair - ((a * (2 * EXPERTS_PER_GROUP - 1 - a)) >> 1) + a + 1
    e_lo = group * EXPERTS_PER_GROUP + a
    e_hi = group * EXPERTS_PER_GROUP + b
    as_i32 = lambda a: a.astype(jnp.int32)
    return as_i32(pos), tuple(as_i32(a) for a in (tile, e_lo, e_hi, lo, hi, first))


def _moe_visit_kernel(tile_ref, elo_ref, ehi_ref, lo_ref, hi_ref, first_ref, xr_ref, g_ref,
                      w1a_ref, w1b_ref, w3a_ref, w3b_ref, w2a_ref, w2b_ref, gf_ref, o_ref, *,
                      tm, final):
    v = pl.program_id(0)
    lo, hi = lo_ref[v], hi_ref[v]

    @pl.when(hi > lo)
    def _():
        x = xr_ref[:, :D_MODEL]
        info = xr_ref[:, D_MODEL:]
        hb = _rmsnorm(x, g_ref[...]).astype(BF16)

        def hidden(w1_ref, w3_ref, gate):
            a = _dot(hb, w1_ref[0])
            return (a * _sigmoid(a) * _dot(hb, w3_ref[0]) * gate).astype(BF16)

        h_lo = hidden(w1a_ref, w3a_ref, info[:, INFO_W_LO:INFO_W_LO + 1])
        h_hi = hidden(w1b_ref, w3b_ref, info[:, INFO_W_HI:INFO_W_HI + 1])
        res = x + (_dot(h_lo, w2a_ref[0]) + _dot(h_hi, w2b_ref[0]))
        if final:
            res = _rmsnorm(res, gf_ref[...])
        row = tile_ref[v] * tm + lax.broadcasted_iota(jnp.int32, (tm, 1), 0)
        mine = (row >= lo) & (row < hi)

        @pl.when(first_ref[v] == 1)
        def _():
            o_ref[...] = jnp.where(mine, res, 0.0)

        @pl.when(first_ref[v] == 0)
        def _():
            o_ref[...] = jnp.where(mine, res, o_ref[...])


def _moe_visits(xr_sorted, visits, p, final_g, *, tm, final):
    n = xr_sorted.shape[0]
    n_visits = visits[0].shape[0]
    by_tile = lambda width: pl.BlockSpec((tm, width), lambda v, tile, *_: (tile[v], 0))
    expert_lo = lambda shape: pl.BlockSpec((1,) + shape, lambda v, tile, elo, *_: (elo[v], 0, 0))
    expert_hi = lambda shape: pl.BlockSpec((1,) + shape,
                                           lambda v, tile, elo, ehi, *_: (ehi[v], 0, 0))
    up, down = (D_MODEL, D_EXPERT), (D_EXPERT, D_MODEL)
    return pl.pallas_call(
        functools.partial(_moe_visit_kernel, tm=tm, final=final),
        grid_spec=pltpu.PrefetchScalarGridSpec(
            num_scalar_prefetch=len(visits),
            grid=(n_visits,),
            in_specs=[by_tile(ROUTED_WIDTH), _resident(p['norm3_g'].shape),
                      expert_lo(up), expert_hi(up), expert_lo(up), expert_hi(up),
                      expert_lo(down), expert_hi(down), _resident(final_g.shape)],
            out_specs=by_tile(D_MODEL)),
        out_shape=jax.ShapeDtypeStruct((n, D_MODEL), F32),
        compiler_params=pltpu.CompilerParams(
            dimension_semantics=("arbitrary",), vmem_limit_bytes=VMEM_LIMIT_BYTES),
        name="moe_visits",
    )(*visits, xr_sorted, p['norm3_g'], p['w1'], p['w1'], p['w3'], p['w3'], p['w2'], p['w2'],
      final_g)


def _routed_moe(flat, p, final_g, *, rank_tile, tm, final):
    cls = flat[:, D_MODEL + INFO_CLASS].astype(jnp.int32)
    rank = flat[:, D_MODEL + INFO_RANK].astype(jnp.int32)
    pos, visits = _routing_plan(cls, rank, rank_tile=rank_tile, tm=tm)
    xr_sorted = _permute_rows(flat, pos, rows=PERMUTE_ROWS, gather=False)
    y_sorted = _moe_visits(xr_sorted, visits, p, final_g, tm=tm, final=final)
    return _permute_rows(y_sorted, pos, rows=PERMUTE_ROWS, gather=True)


def _split_bf16(w):
    hi = w.astype(BF16)
    lo = (w - hi.astype(F32)).astype(BF16)
    return jnp.concatenate([hi, lo], axis=1)


def _layer_params(l, norm1_g, w_in, conv_a_w, conv_a_b, w_a_out, ln_v_g, ln_v_b, w_s, b_s,
                  w_b_out, conv_c_w, conv_c_b, ln_c_g, ln_c_b, w_c_out, w_o, norm2_g,
                  w_q, w_mo, norm3_g, rg_w, rg_b, re_w, re_b, w1, w3, w2):
    row = lambda a: a[l].reshape(1, -1)
    bf = lambda a: a[l].astype(BF16)
    w_router = jnp.concatenate([rg_w[l], re_w[l].reshape(D_MODEL, N_EXPERTS)], axis=1)
    b_router = jnp.concatenate([rg_b[l], re_b[l].reshape(N_EXPERTS)])
    pad = ROUTER_COLS - MOE_GROUPS - N_EXPERTS
    return dict(
        norm1_g=row(norm1_g), w_in=bf(w_in), conv_a_w=conv_a_w[l], conv_a_b=row(conv_a_b),
        w_a_out=bf(w_a_out), ln_v_g=row(ln_v_g), ln_v_b=row(ln_v_b), w_s=w_s[l], b_s=b_s[l],
        w_b_out=bf(w_b_out), conv_c_w=conv_c_w[l], conv_c_b=row(conv_c_b), ln_c_g=row(ln_c_g),
        ln_c_b=row(ln_c_b), w_c_out=bf(w_c_out), w_o=bf(w_o), norm2_g=row(norm2_g),
        w_q=bf(w_q), w_mo=bf(w_mo),
        norm3_g=row(norm3_g), w_router=_split_bf16(jnp.pad(w_router, ((0, 0), (0, pad)))),
        b_router=jnp.pad(b_router, (0, pad)).reshape(1, -1),
        w_router_t=_split_bf16(jnp.pad(w_router, ((0, 0), (0, pad)))).T,
        b_router_t=jnp.pad(b_router, (0, ROUTE_ROWS - MOE_GROUPS - N_EXPERTS)).reshape(-1, 1),
        w1=bf(w1), w3=bf(w3), w2=bf(w2))


def kernel(x_prompt, x_sample, state_conv_a, state_conv_c, cache_mem_k, cache_mem_v, mem_prompt, norm1_g, w_in, conv_a_w, conv_a_b, w_a_out, ln_v_g, ln_v_b, w_s, b_s, w_b_out, conv_c_w, conv_c_b, ln_c_g, ln_c_b, w_c_out, w_o, norm2_g, mem_norm_g, w_q, w_k, w_v, w_mo, norm3_g, rg_w, rg_b, re_w, re_b, w1, w3, w2, final_norm_g):
    depth = w_in.shape[0]
    n_p, n_s = x_prompt.shape[0], x_sample.shape[0]
    final_g = final_norm_g.reshape(1, -1)
    x_p, x_s = x_prompt, x_sample
    zero_a = jnp.zeros((n_p, CONV_A - 1, D_A), F32)
    zero_c = jnp.zeros((n_p, CONV_C - 1, D_C), F32)
    sa_p, sa_s, sc_p, sc_s, v_s = [], [], [], [], []
    mk, mv, mk_heads, mv_heads = _memory_kv(mem_prompt, mem_norm_g, w_k, w_v, tm=512)
    mk = mk.reshape(depth, n_p, N_MEM, D_MODEL)
    mv = mv.reshape(depth, n_p, N_MEM, D_MODEL)
    ck = cache_mem_k.reshape(depth, n_s, N_MEM, D_MODEL)
    cv = cache_mem_v.reshape(depth, n_s, N_MEM, D_MODEL)
    for l in range(depth):
        p = _layer_params(l, norm1_g, w_in, conv_a_w, conv_a_b, w_a_out, ln_v_g, ln_v_b, w_s,
                          b_s, w_b_out, conv_c_w, conv_c_b, ln_c_g, ln_c_b, w_c_out, w_o,
                          norm2_g, w_q, w_mo, norm3_g, rg_w, rg_b, re_w,
                          re_b, w1, w3, w2)
        final = l == depth - 1
        x_p, na, nc = _token_mix(x_p, zero_a, zero_c, p, tl=512, emit_v=False)
        xr_p = _cross_attend(x_p, mk, mv, l, p, tl=512, route=True)
        x_p = _routed_moe(xr_p, p, final_g, rank_tile=512, tm=512, final=final).reshape(x_p.shape)
        sa_p.append(na)
        sc_p.append(nc)
        x_s, na, nc, vr = _token_mix(x_s, state_conv_a[l], state_conv_c[l], p, tl=x_s.shape[1],
                                     emit_v=True)
        x_s = _cross_attend(x_s, ck, cv, l, p, tl=x_s.shape[1], route=False)
        x_s = _hier_moe(x_s, p, final_g, tm=x_s.shape[0] * x_s.shape[1], final=final)
        sa_s.append(na)
        sc_s.append(nc)
        v_s.append(vr)
    return (x_p, x_s, jnp.stack(sa_p), jnp.stack(sa_s), jnp.stack(sc_p), jnp.stack(sc_s),
            jnp.stack(v_s), mk_heads.reshape(depth, n_p, N_MEM, MEM_HEADS, MEM_HEAD_DIM),
            mv_heads.reshape(depth, n_p, N_MEM, MEM_HEADS, MEM_HEAD_DIM))
```

```python
import functools

import jax
import jax.numpy as jnp
from jax import lax
from jax.experimental import pallas as pl
from jax.experimental.pallas import tpu as pltpu

D_MODEL = 1024
D_A = D_MODEL // 2
CONV_A = 3
D_B = D_MODEL // 2
B_GROUPS = 4
B_GROUP_DIM = D_B // B_GROUPS
B_CHUNK = 128
D_C = D_MODEL // 2
CONV_C = 31
N_MEM = 256
MEM_HEADS = 4
MEM_HEAD_DIM = D_MODEL // MEM_HEADS
MOE_GROUPS = 4
EXPERTS_PER_GROUP = 4
N_EXPERTS = MOE_GROUPS * EXPERTS_PER_GROUP
D_EXPERT = D_MODEL // 4
EPS = 1e-6
COL_XA = 0
COL_BA = COL_XA + D_A
COL_CA = COL_BA + D_A
COL_UV = COL_CA + D_A
COL_GLU = COL_UV + 2 * D_B
COL_GATE = COL_GLU + 2 * D_C
D_IN = COL_GATE + 3 * D_MODEL

SUBLANES = 8
LANES = 128
VMEM_LIMIT_BYTES = 56 * 1024 * 1024
DMA_QUEUES = 2

PA_NEW = SUBLANES
PA_HIST = PA_NEW - (CONV_A - 1)
PC_NEW = 32
PC_HIST = PC_NEW - (CONV_C - 1)
CONV_ROWS = 64
ROUTER_COLS = LANES
COL_EXPERT0 = MOE_GROUPS
ROUTE_ROWS = 32
PAIRS_PER_GROUP = EXPERTS_PER_GROUP * (EXPERTS_PER_GROUP - 1) // 2
N_CLASSES = MOE_GROUPS * PAIRS_PER_GROUP
INFO_CLASS, INFO_RANK, INFO_W_LO, INFO_W_HI = 0, 1, 2, 3
ROUTED_WIDTH = D_MODEL + ROUTER_COLS
PERMUTE_ROWS = 1024

BF16 = jnp.bfloat16
F32 = jnp.float32


def _dot(a, b):
    return jnp.dot(a, b, preferred_element_type=F32)


def _rmsnorm(x, g):
    ms = jnp.mean(x * x, axis=-1, keepdims=True)
    return x * lax.rsqrt(ms + EPS) * g


def _layernorm(x, g, b):
    xc = x - jnp.mean(x, axis=-1, keepdims=True)
    var = jnp.mean(xc * xc, axis=-1, keepdims=True)
    return xc * lax.rsqrt(var + EPS) * g + b


def _sigmoid(x):
    return 0.5 * jnp.tanh(0.5 * x) + 0.5


def _resident(shape):
    nd = len(shape)
    return pl.BlockSpec(shape, lambda *_: (0,) * nd, pipeline_mode=pl.Buffered(1))


def _mix_kernel(x_ref, sa_ref, sc_ref, g1_ref, win_ref, caw_ref, cab_ref, wa_ref, lvg_ref,
                lvb_ref, ws_ref, bs_ref, wb_ref, ccw_ref, ccb_ref, lcg_ref, lcb_ref, wc_ref,
                wo_ref, xo_ref, na_ref, nc_ref, *rest, tl, cl, emit_v):
    v_ref = rest[0] if emit_v else None
    pa_buf, pc_buf, s_buf, act_buf = rest[-4:]
    t = pl.program_id(1)

    @pl.when(t == 0)
    def _():
        pa_buf[PA_HIST:PA_NEW, :] = sa_ref[0]
        pc_buf[PC_HIST:PC_NEW, :] = sc_ref[0]

    x = x_ref[0]
    hb = _rmsnorm(x, g1_ref[...]).astype(BF16)

    def gate(i):
        lo = COL_GATE + i * D_MODEL
        return _sigmoid(_dot(hb, win_ref[:, lo:lo + D_MODEL]))

    zg = _dot(hb, win_ref[:, COL_GLU:COL_GATE])
    pc_buf[PC_NEW:PC_NEW + tl, :] = zg[:, :D_C] * _sigmoid(zg[:, D_C:])
    for r in range(tl // CONV_ROWS):
        base = PC_HIST + r * CONV_ROWS
        acc = jnp.broadcast_to(ccb_ref[...], (CONV_ROWS, D_C))
        for off in range(SUBLANES):
            part = None
            for k in range(CONV_C):
                if (base + k) % SUBLANES == off:
                    term = ccw_ref[k:k + 1, :] * pc_buf[base + k:base + k + CONV_ROWS, :]
                    part = term if part is None else part + term
            acc = acc + part
        ln = _layernorm(acc, lcg_ref[...], lcb_ref[...])
        act_buf[r * CONV_ROWS:(r + 1) * CONV_ROWS, :] = (ln * _sigmoid(ln)).astype(BF16)
    new_c = pc_buf[PC_HIST + tl:PC_NEW + tl, :]
    nc_ref[0] = new_c
    pc_buf[PC_HIST:PC_NEW, :] = new_c

    za = _dot(hb, win_ref[:, COL_XA:COL_UV])
    pa_buf[PA_NEW:PA_NEW + tl, :] = za[:, 2 * D_A:] * za[:, :D_A]
    conv_a = cab_ref[...]
    for k in range(CONV_A):
        conv_a = conv_a + caw_ref[k:k + 1, :] * pa_buf[PA_HIST + k:PA_HIST + k + tl, :]
    gates = [gate(i) for i in range(3)]
    y_a = _dot((za[:, D_A:2 * D_A] * conv_a).astype(BF16), wa_ref[...])
    merged = gates[0] * y_a
    new_a = pa_buf[PA_HIST + tl:PA_NEW + tl, :]
    na_ref[0] = new_a
    pa_buf[PA_HIST:PA_NEW, :] = new_a

    zuv = _dot(hb, win_ref[:, COL_UV:COL_GLU])
    uv = 0.5 * zuv * (1.0 + lax.erf(zuv * (0.5 ** 0.5)))
    v = _layernorm(uv[:, D_B:], lvg_ref[...], lvb_ref[...])
    if emit_v:
        v_ref[0] = v
    vb = v.astype(BF16)
    row = lax.broadcasted_iota(jnp.int32, (cl, cl), 0)
    col = lax.broadcasted_iota(jnp.int32, (cl, cl), 1)
    for g in range(B_GROUPS):
        wg = jnp.where(row >= col, ws_ref[g], 0.0).astype(BF16)
        for c in range(tl // cl):
            blk = vb[c * cl:(c + 1) * cl, g * B_GROUP_DIM:(g + 1) * B_GROUP_DIM]
            s_buf[c * cl:(c + 1) * cl, g * B_GROUP_DIM:(g + 1) * B_GROUP_DIM] = (
                _dot(wg, blk) + bs_ref[:, g * B_GROUP_DIM:(g + 1) * B_GROUP_DIM])
    y_b = _dot((uv[:, :D_B] * s_buf[...]).astype(BF16), wb_ref[...])
    merged = merged + gates[1] * y_b

    y_c = _dot(act_buf[...], wc_ref[...])
    merged = merged + gates[2] * y_c

    xo_ref[0] = x + _dot(merged.astype(BF16), wo_ref[...])


def _token_mix(x, state_a, state_c, p, *, tl, emit_v):
    n, L, _ = x.shape
    cl = min(L, B_CHUNK)
    assert L % tl == 0 and tl % cl == 0 and tl % CONV_ROWS == 0 and tl >= CONV_C - 1
    ws = p['w_s'][:, :cl, :cl]
    bs = jnp.repeat(p['b_s'][:, :cl].T, B_GROUP_DIM, axis=1)
    weights = [p['norm1_g'], p['w_in'], p['conv_a_w'], p['conv_a_b'], p['w_a_out'],
               p['ln_v_g'], p['ln_v_b'], ws, bs, p['w_b_out'], p['conv_c_w'], p['conv_c_b'],
               p['ln_c_g'], p['ln_c_b'], p['w_c_out'], p['w_o']]
    tile = lambda width: pl.BlockSpec((1, tl, width), lambda b, t: (b, t, 0))
    per_seq = lambda rows, width: pl.BlockSpec((1, rows, width), lambda b, t: (b, 0, 0))
    return pl.pallas_call(
        functools.partial(_mix_kernel, tl=tl, cl=cl, emit_v=emit_v),
        grid=(n, L // tl),
        in_specs=[tile(D_MODEL), per_seq(CONV_A - 1, D_A), per_seq(CONV_C - 1, D_C)]
        + [_resident(w.shape) for w in weights],
        out_specs=[tile(D_MODEL), per_seq(CONV_A - 1, D_A), per_seq(CONV_C - 1, D_C)]
        + ([tile(D_B)] if emit_v else []),
        out_shape=[jax.ShapeDtypeStruct((n, L, D_MODEL), F32),
                   jax.ShapeDtypeStruct((n, CONV_A - 1, D_A), F32),
                   jax.ShapeDtypeStruct((n, CONV_C - 1, D_C), F32)]
        + ([jax.ShapeDtypeStruct((n, L, D_B), F32)] if emit_v else []),
        scratch_shapes=[pltpu.VMEM((PA_NEW + tl, D_A), F32),
                        pltpu.VMEM((PC_NEW + tl, D_C), F32),
                        pltpu.VMEM((tl, D_B), F32),
                        pltpu.VMEM((tl, D_C), BF16)],
        compiler_params=pltpu.CompilerParams(
            dimension_semantics=("parallel", "arbitrary"), vmem_limit_bytes=VMEM_LIMIT_BYTES),
        name="token_mix",
    )(x, state_a, state_c, *weights)


def _memkv_kernel(m_ref, g_ref, wk_ref, wv_ref, kf_ref, vf_ref, kh_ref, vh_ref):
    mb = _rmsnorm(m_ref[...], g_ref[0]).astype(BF16)
    for w_ref, flat_ref, heads_ref in ((wk_ref, kf_ref, kh_ref), (wv_ref, vf_ref, vh_ref)):
        proj = _dot(mb, w_ref[0])
        flat_ref[0] = proj
        for h in range(MEM_HEADS):
            heads_ref[0, :, h, :] = proj[:, h * MEM_HEAD_DIM:(h + 1) * MEM_HEAD_DIM]


def _memory_kv(mem, mem_norm_g, w_k, w_v, *, tm):
    depth = w_k.shape[0]
    n, m, _ = mem.shape
    rows = n * m
    assert rows % tm == 0
    flat = pl.BlockSpec((1, tm, D_MODEL), lambda l, i: (l, i, 0))
    heads = pl.BlockSpec((1, tm, MEM_HEADS, MEM_HEAD_DIM), lambda l, i: (l, i, 0, 0))
    weight = pl.BlockSpec((1, D_MODEL, D_MODEL), lambda l, i: (l, 0, 0))
    return pl.pallas_call(
        _memkv_kernel,
        grid=(depth, rows // tm),
        in_specs=[pl.BlockSpec((tm, D_MODEL), lambda l, i: (i, 0)),
                  pl.BlockSpec((1, 1, D_MODEL), lambda l, i: (l, 0, 0)), weight, weight],
        out_specs=[flat, flat, heads, heads],
        out_shape=[jax.ShapeDtypeStruct((depth, rows, D_MODEL), F32)] * 2
        + [jax.ShapeDtypeStruct((depth, rows, MEM_HEADS, MEM_HEAD_DIM), F32)] * 2,
        compiler_params=pltpu.CompilerParams(
            dimension_semantics=("parallel", "parallel"), vmem_limit_bytes=VMEM_LIMIT_BYTES),
        name="memory_kv",
    )(mem.reshape(rows, D_MODEL), mem_norm_g.reshape(depth, 1, D_MODEL), w_k.astype(BF16),
      w_v.astype(BF16))


def _attention(x, k_ref, v_ref, g_ref, wq_ref, wmo_ref, o_buf):
    hb = _rmsnorm(x, g_ref[...]).astype(BF16)
    qb = (_dot(hb, wq_ref[...]) * (MEM_HEAD_DIM ** -0.5)).astype(BF16)
    heads = [slice(h * MEM_HEAD_DIM, (h + 1) * MEM_HEAD_DIM) for h in range(MEM_HEADS)]
    scores = [lax.dot_general(qb[:, cols], k_ref[0, 0, :, cols].astype(BF16),
                              (((1,), (1,)), ((), ())), preferred_element_type=F32)
              for cols in heads]
    for s, cols in zip(scores, heads):
        e = jnp.exp(s - jnp.max(s, axis=-1, keepdims=True))
        prob = e * (1.0 / jnp.sum(e, axis=-1, keepdims=True))
        o_buf[:, cols] = _dot(prob.astype(BF16), v_ref[0, 0, :, cols].astype(BF16)).astype(BF16)
    return x + _dot(o_buf[...], wmo_ref[...])


def _attn_kernel(x_ref, k_ref, v_ref, g_ref, wq_ref, wmo_ref, *rest, route):
    xo_ref, o_buf = rest[-2:]
    xn = _attention(x_ref[0], k_ref, v_ref, g_ref, wq_ref, wmo_ref, o_buf)
    if route:
        g3_ref, wr_ref, br_ref = rest[:3]
        xo_ref[:, :D_MODEL] = xn
        xo_ref[:, D_MODEL:] = _route_info(xn, g3_ref, wr_ref, br_ref)
    else:
        xo_ref[0] = xn


def _cross_attend(x, mem_k, mem_v, layer, p, *, tl, route):
    n, L, _ = x.shape
    assert L % tl == 0
    per_seq_tiles = L // tl
    tile = pl.BlockSpec((1, tl, D_MODEL), lambda b, t: (b, t, 0))
    per_seq = pl.BlockSpec((1, 1, N_MEM, D_MODEL), lambda b, t: (layer, b, 0, 0))
    weights = [p['norm2_g'], p['w_q'], p['w_mo']]
    if route:
        weights += [p['norm3_g'], p['w_router_t'], p['b_router_t']]
        out_spec = pl.BlockSpec((tl, ROUTED_WIDTH), lambda b, t: (b * per_seq_tiles + t, 0))
        out_shape = jax.ShapeDtypeStruct((n * L, ROUTED_WIDTH), F32)
    else:
        out_spec, out_shape = tile, jax.ShapeDtypeStruct((n, L, D_MODEL), F32)
    return pl.pallas_call(
        functools.partial(_attn_kernel, route=route),
        grid=(n, per_seq_tiles),
        in_specs=[tile, per_seq, per_seq] + [_resident(w.shape) for w in weights],
        out_specs=out_spec,
        out_shape=out_shape,
        scratch_shapes=[pltpu.VMEM((tl, D_MODEL), BF16)],
        compiler_params=pltpu.CompilerParams(
            dimension_semantics=("parallel", "arbitrary"), vmem_limit_bytes=VMEM_LIMIT_BYTES),
        name="cross_attend",
    )(x, mem_k, mem_v, *weights)


def _router_logits(x, g_ref, wr_ref, br_ref):
    h = _rmsnorm(x, g_ref[...])
    h_hi = h.astype(BF16)
    h_lo = (h - h_hi.astype(F32)).astype(BF16)
    by_hi = _dot(h_hi, wr_ref[...])
    by_lo = _dot(h_lo, wr_ref[:, :ROUTER_COLS])
    return by_hi[:, :ROUTER_COLS] + (by_hi[:, ROUTER_COLS:] + by_lo) + br_ref[...]


def _route(logits, axis):
    idx = lax.broadcasted_iota(jnp.int32, logits.shape, axis)
    along = lambda f, a: f(a, axis=axis, keepdims=True)
    neg = -jnp.inf
    big = jnp.int32(logits.shape[axis])
    lg = jnp.where(idx < MOE_GROUPS, logits, neg)
    mg = along(jnp.max, lg)
    p_top = 1.0 / along(jnp.sum, jnp.exp(lg - mg))
    g_idx = along(jnp.min, jnp.where(lg == mg, idx, big))
    expert = idx - COL_EXPERT0
    in_group = (expert >= 0) & (expert < N_EXPERTS) & ((expert >> 2) == g_idx)
    le = jnp.where(in_group, logits, neg)
    m1 = along(jnp.max, le)
    i1 = along(jnp.min, jnp.where(le == m1, idx, big))
    le2 = jnp.where(idx == i1, neg, le)
    m2 = along(jnp.max, le2)
    i2 = along(jnp.min, jnp.where(le2 == m2, idx, big))
    e2 = jnp.exp(m2 - m1)
    w1 = p_top / (1.0 + e2)
    w2 = p_top * e2 / (1.0 + e2)
    return idx, g_idx, i1, i2, w1, w2


def _expert_gates(logits):
    col, _, i1, i2, w1, w2 = _route(logits, 1)
    return jnp.where(col == i1, w1, 0.0) + jnp.where(col == i2, w2, 0.0)


def _route_info(x, g_ref, wrt_ref, brt_ref):
    tm = x.shape[0]
    h = _rmsnorm(x, g_ref[...])
    h_hi = h.astype(BF16)
    h_lo = (h - h_hi.astype(F32)).astype(BF16)
    contract_last = (((1,), (1,)), ((), ()))
    by_hi = lax.dot_general(wrt_ref[...], h_hi, contract_last, preferred_element_type=F32)
    by_lo = lax.dot_general(wrt_ref[:ROUTER_COLS, :], h_lo, contract_last,
                            preferred_element_type=F32)
    logits = by_hi[:ROUTER_COLS] + (by_hi[ROUTER_COLS:] + by_lo)
    logits = logits[:ROUTE_ROWS] + brt_ref[...]

    _, g_idx, i1, i2, w1, w2 = _route(logits, 0)
    first_is_lo = i1 < i2
    a = (jnp.minimum(i1, i2) - COL_EXPERT0) & (EXPERTS_PER_GROUP - 1)
    b = (jnp.maximum(i1, i2) - COL_EXPERT0) & (EXPERTS_PER_GROUP - 1)
    cls = g_idx * PAIRS_PER_GROUP + ((a * (2 * EXPERTS_PER_GROUP - 1 - a)) >> 1) + (b - a - 1)
    class_row = lax.broadcasted_iota(jnp.int32, (ROUTE_ROWS, tm), 0)
    onehot = jnp.where(class_row == cls, 1.0, 0.0)
    earlier = lax.broadcasted_iota(jnp.int32, (tm, tm), 0)
    token = lax.broadcasted_iota(jnp.int32, (tm, tm), 1)
    upto = jnp.where(earlier <= token, 1.0, 0.0).astype(BF16)
    running = _dot(onehot.astype(BF16), upto)
    rank = jnp.sum(running * onehot, axis=0, keepdims=True) - 1.0
    w_lo = jnp.where(first_is_lo, w1, w2)
    w_hi = jnp.where(first_is_lo, w2, w1)
    field = lax.broadcasted_iota(jnp.int32, (ROUTER_COLS, tm), 0)
    record = jnp.where(field == INFO_CLASS, cls.astype(F32),
                       jnp.where(field == INFO_RANK, rank,
                                 jnp.where(field == INFO_W_LO, w_lo,
                                           jnp.where(field == INFO_W_HI, w_hi, 0.0))))
    return record.T


def _moe_kernel(x_ref, g_ref, wr_ref, br_ref, w1_ref, w3_ref, w2_ref, gf_ref, o_ref, *, final):
    x = x_ref[...]
    gates = _expert_gates(_router_logits(x, g_ref, wr_ref, br_ref))
    hb = _rmsnorm(x, g_ref[...]).astype(BF16)
    acc = x
    for e in range(N_EXPERTS):
        a = _dot(hb, w1_ref[e])
        hid = a * _sigmoid(a) * _dot(hb, w3_ref[e])
        g = gates[:, COL_EXPERT0 + e:COL_EXPERT0 + e + 1]
        acc = acc + _dot((hid * g).astype(BF16), w2_ref[e])
    o_ref[...] = _rmsnorm(acc, gf_ref[...]) if final else acc


def _hier_moe(x, p, final_g, *, tm, final):
    shape = x.shape
    t = x.reshape(-1, D_MODEL)
    rows = t.shape[0]
    assert rows % tm == 0
    tile = pl.BlockSpec((tm, D_MODEL), lambda i: (i, 0))
    weights = [p['norm3_g'], p['w_router'], p['b_router'], p['w1'], p['w3'], p['w2'], final_g]
    out = pl.pallas_call(
        functools.partial(_moe_kernel, final=final),
        grid=(rows // tm,),
        in_specs=[tile] + [_resident(w.shape) for w in weights],
        out_specs=tile,
        out_shape=jax.ShapeDtypeStruct((rows, D_MODEL), F32),
        compiler_params=pltpu.CompilerParams(
            dimension_semantics=("parallel",), vmem_limit_bytes=VMEM_LIMIT_BYTES),
        name="hier_moe",
    )(t, *weights)
    return out.reshape(shape)


def _permute_kernel(pos_ref, src_ref, dst_ref, sem, *, rows, gather):
    def row_copy(j):
        p = pos_ref[0, 0, j]
        if gather:
            return pltpu.make_async_copy(src_ref.at[pl.ds(p, 1)], dst_ref.at[pl.ds(j, 1)], sem)
        return pltpu.make_async_copy(src_ref.at[pl.ds(j, 1)], dst_ref.at[pl.ds(p, 1)], sem)

    for j in range(rows):
        row_copy(j).start(priority=j % DMA_QUEUES)

    def wait(j, carry):
        row_copy(0).wait()
        return carry

    lax.fori_loop(0, rows, wait, 0, unroll=8)


def _permute_rows(src, pos, *, rows, gather):
    n, width = src.shape
    assert n % rows == 0
    steps = n // rows
    tile = pl.BlockSpec((rows, width), lambda s: (s, 0))
    hbm = pl.BlockSpec(memory_space=pl.ANY)
    return pl.pallas_call(
        functools.partial(_permute_kernel, rows=rows, gather=gather),
        grid=(steps,),
        in_specs=[pl.BlockSpec((1, 1, rows), lambda s: (s, 0, 0), memory_space=pltpu.SMEM),
                  hbm if gather else tile],
        out_specs=tile if gather else hbm,
        out_shape=jax.ShapeDtypeStruct((n, width), src.dtype),
        scratch_shapes=[pltpu.SemaphoreType.DMA(())],
        compiler_params=pltpu.CompilerParams(dimension_semantics=("arbitrary",)),
        name="gather_rows" if gather else "scatter_rows",
    )(pos.reshape(steps, 1, rows), src)


def _routing_plan(cls, rank, *, rank_tile, tm):
    n = cls.shape[0]
    tiles = n // tm
    rank_tiles = n // rank_tile
    onehot = cls.reshape(rank_tiles, rank_tile, 1) == jnp.arange(N_CLASSES, dtype=jnp.int32)
    counts = jnp.sum(onehot.astype(jnp.int32), axis=1)
    class_end = jnp.cumsum(jnp.sum(counts, axis=0))
    class_start = class_end - jnp.sum(counts, axis=0)
    tile_base = class_start[None, :] + jnp.cumsum(counts, axis=0) - counts
    pos = jnp.sum(jnp.where(onehot, tile_base[:, None, :], 0), axis=2).reshape(n) + rank
    tile_cut = jnp.arange(tiles, dtype=jnp.int32) * tm
    class_cut = class_start[1:]
    tile_slot = jnp.arange(tiles) + jnp.sum(class_cut[None, :] < tile_cut[:, None], axis=1)
    class_slot = (jnp.arange(N_CLASSES - 1)
                  + jnp.sum(tile_cut[None, :] <= class_cut[:, None], axis=1))
    slots = jnp.arange(tiles + N_CLASSES - 1)[:, None]
    lo = (jnp.sum(jnp.where(tile_slot[None, :] == slots, tile_cut[None, :], 0), axis=1)
          + jnp.sum(jnp.where(class_slot[None, :] == slots, class_cut[None, :], 0), axis=1))
    hi = jnp.concatenate([lo[1:], jnp.array([n], lo.dtype)])
    tile = jnp.minimum(lo // tm, tiles - 1)
    c = jnp.minimum(jnp.sum(class_end[None, :] <= lo[:, None], axis=1), N_CLASSES - 1)
    first = ((hi > lo) & (lo == tile * tm)).astype(jnp.int32)
    group, pair = c // PAIRS_PER_GROUP, c % PAIRS_PER_GROUP
    a = (pair >= EXPERTS_PER_GROUP - 1).astype(jnp.int32) + (pair >= 2 * EXPERTS_PER_GROUP - 3)
    b = pair - ((a * (2 * EXPERTS_PER_GROUP - 1 - a)) >> 1) + a + 1
    e_lo = group * EXPERTS_PER_GROUP + a
    e_hi = group * EXPERTS_PER_GROUP + b
    as_i32 = lambda a: a.astype(jnp.int32)
    return as_i32(pos), tuple(as_i32(a) for a in (tile, e_lo, e_hi, lo, hi, first))


def _moe_visit_kernel(tile_ref, elo_ref, ehi_ref, lo_ref, hi_ref, first_ref, xr_ref, g_ref,
                      w1a_ref, w1b_ref, w3a_ref, w3b_ref, w2a_ref, w2b_ref, gf_ref, o_ref, *,
                      tm, final):
    v = pl.program_id(0)
    lo, hi = lo_ref[v], hi_ref[v]

    @pl.when(hi > lo)
    def _():
        x = xr_ref[:, :D_MODEL]
        info = xr_ref[:, D_MODEL:]
        hb = _rmsnorm(x, g_ref[...]).astype(BF16)

        def hidden(w1_ref, w3_ref, gate):
            a = _dot(hb, w1_ref[0])
            return (a * _sigmoid(a) * _dot(hb, w3_ref[0]) * gate).astype(BF16)

        h_lo = hidden(w1a_ref, w3a_ref, info[:, INFO_W_LO:INFO_W_LO + 1])
        h_hi = hidden(w1b_ref, w3b_ref, info[:, INFO_W_HI:INFO_W_HI + 1])
        res = x + (_dot(h_lo, w2a_ref[0]) + _dot(h_hi, w2b_ref[0]))
        if final:
            res = _rmsnorm(res, gf_ref[...])
        row = tile_ref[v] * tm + lax.broadcasted_iota(jnp.int32, (tm, 1), 0)
        mine = (row >= lo) & (row < hi)

        @pl.when(first_ref[v] == 1)
        def _():
            o_ref[...] = jnp.where(mine, res, 0.0)

        @pl.when(first_ref[v] == 0)
        def _():
            o_ref[...] = jnp.where(mine, res, o_ref[...])


def _moe_visits(xr_sorted, visits, p, final_g, *, tm, final):
    n = xr_sorted.shape[0]
    n_visits = visits[0].shape[0]
    by_tile = lambda width: pl.BlockSpec((tm, width), lambda v, tile, *_: (tile[v], 0))
    expert_lo = lambda shape: pl.BlockSpec((1,) + shape, lambda v, tile, elo, *_: (elo[v], 0, 0))
    expert_hi = lambda shape: pl.BlockSpec((1,) + shape,
                                           lambda v, tile, elo, ehi, *_: (ehi[v], 0, 0))
    up, down = (D_MODEL, D_EXPERT), (D_EXPERT, D_MODEL)
    return pl.pallas_call(
        functools.partial(_moe_visit_kernel, tm=tm, final=final),
        grid_spec=pltpu.PrefetchScalarGridSpec(
            num_scalar_prefetch=len(visits),
            grid=(n_visits,),
            in_specs=[by_tile(ROUTED_WIDTH), _resident(p['norm3_g'].shape),
                      expert_lo(up), expert_hi(up), expert_lo(up), expert_hi(up),
                      expert_lo(down), expert_hi(down), _resident(final_g.shape)],
            out_specs=by_tile(D_MODEL)),
        out_shape=jax.ShapeDtypeStruct((n, D_MODEL), F32),
        compiler_params=pltpu.CompilerParams(
            dimension_semantics=("arbitrary",), vmem_limit_bytes=VMEM_LIMIT_BYTES),
        name="moe_visits",
    )(*visits, xr_sorted, p['norm3_g'], p['w1'], p['w1'], p['w3'], p['w3'], p['w2'], p['w2'],
      final_g)


def _routed_moe(flat, p, final_g, *, rank_tile, tm, final):
    cls = flat[:, D_MODEL + INFO_CLASS].astype(jnp.int32)
    rank = flat[:, D_MODEL + INFO_RANK].astype(jnp.int32)
    pos, visits = _routing_plan(cls, rank, rank_tile=rank_tile, tm=tm)
    xr_sorted = _permute_rows(flat, pos, rows=PERMUTE_ROWS, gather=False)
    y_sorted = _moe_visits(xr_sorted, visits, p, final_g, tm=tm, final=final)
    return _permute_rows(y_sorted, pos, rows=PERMUTE_ROWS, gather=True)


def _split_bf16(w):
    hi = w.astype(BF16)
    lo = (w - hi.astype(F32)).astype(BF16)
    return jnp.concatenate([hi, lo], axis=1)


def _layer_params(l, norm1_g, w_in, conv_a_w, conv_a_b, w_a_out, ln_v_g, ln_v_b, w_s, b_s,
                  w_b_out, conv_c_w, conv_c_b, ln_c_g, ln_c_b, w_c_out, w_o, norm2_g,
                  w_q, w_mo, norm3_g, rg_w, rg_b, re_w, re_b, w1, w3, w2):
    row = lambda a: a[l].reshape(1, -1)
    bf = lambda a: a[l].astype(BF16)
    w_router = jnp.concatenate([rg_w[l], re_w[l].reshape(D_MODEL, N_EXPERTS)], axis=1)
    b_router = jnp.concatenate([rg_b[l], re_b[l].reshape(N_EXPERTS)])
    pad = ROUTER_COLS - MOE_GROUPS - N_EXPERTS
    return dict(
        norm1_g=row(norm1_g), w_in=bf(w_in), conv_a_w=conv_a_w[l], conv_a_b=row(conv_a_b),
        w_a_out=bf(w_a_out), ln_v_g=row(ln_v_g), ln_v_b=row(ln_v_b), w_s=w_s[l], b_s=b_s[l],
        w_b_out=bf(w_b_out), conv_c_w=conv_c_w[l], conv_c_b=row(conv_c_b), ln_c_g=row(ln_c_g),
        ln_c_b=row(ln_c_b), w_c_out=bf(w_c_out), w_o=bf(w_o), norm2_g=row(norm2_g),
        w_q=bf(w_q), w_mo=bf(w_mo),
        norm3_g=row(norm3_g), w_router=_split_bf16(jnp.pad(w_router, ((0, 0), (0, pad)))),
        b_router=jnp.pad(b_router, (0, pad)).reshape(1, -1),
        w_router_t=_split_bf16(jnp.pad(w_router, ((0, 0), (0, pad)))).T,
        b_router_t=jnp.pad(b_router, (0, ROUTE_ROWS - MOE_GROUPS - N_EXPERTS)).reshape(-1, 1),
        w1=bf(w1), w3=bf(w3), w2=bf(w2))


def kernel(x_prompt, x_sample, state_conv_a, state_conv_c, cache_mem_k, cache_mem_v, mem_prompt, norm1_g, w_in, conv_a_w, conv_a_b, w_a_out, ln_v_g, ln_v_b, w_s, b_s, w_b_out, conv_c_w, conv_c_b, ln_c_g, ln_c_b, w_c_out, w_o, norm2_g, mem_norm_g, w_q, w_k, w_v, w_mo, norm3_g, rg_w, rg_b, re_w, re_b, w1, w3, w2, final_norm_g):
    depth = w_in.shape[0]
    n_p, n_s = x_prompt.shape[0], x_sample.shape[0]
    final_g = final_norm_g.reshape(1, -1)
    x_p, x_s = x_prompt, x_sample
    zero_a = jnp.zeros((n_p, CONV_A - 1, D_A), F32)
    zero_c = jnp.zeros((n_p, CONV_C - 1, D_C), F32)
    sa_p, sa_s, sc_p, sc_s, v_s = [], [], [], [], []
    mk, mv, mk_heads, mv_heads = _memory_kv(mem_prompt, mem_norm_g, w_k, w_v, tm=512)
    mk = mk.reshape(depth, n_p, N_MEM, D_MODEL)
    mv = mv.reshape(depth, n_p, N_MEM, D_MODEL)
    ck = cache_mem_k.reshape(depth, n_s, N_MEM, D_MODEL)
    cv = cache_mem_v.reshape(depth, n_s, N_MEM, D_MODEL)
    for l in range(depth):
        p = _layer_params(l, norm1_g, w_in, conv_a_w, conv_a_b, w_a_out, ln_v_g, ln_v_b, w_s,
                          b_s, w_b_out, conv_c_w, conv_c_b, ln_c_g, ln_c_b, w_c_out, w_o,
                          norm2_g, w_q, w_mo, norm3_g, rg_w, rg_b, re_w,
                          re_b, w1, w3, w2)
        final = l == depth - 1
        x_p, na, nc = _token_mix(x_p, zero_a, zero_c, p, tl=512, emit_v=False)
        xr_p = _cross_attend(x_p, mk, mv, l, p, tl=1024, route=True)
        x_p = _routed_moe(xr_p, p, final_g, rank_tile=1024, tm=512, final=final).reshape(x_p.shape)
        sa_p.append(na)
        sc_p.append(nc)
        x_s, na, nc, vr = _token_mix(x_s, state_conv_a[l], state_conv_c[l], p, tl=x_s.shape[1],
                                     emit_v=True)
        x_s = _cross_attend(x_s, ck, cv, l, p, tl=x_s.shape[1], route=False)
        x_s = _hier_moe(x_s, p, final_g, tm=x_s.shape[0] * x_s.shape[1], final=final)
        sa_s.append(na)
        sc_s.append(nc)
        v_s.append(vr)
    return (x_p, x_s, jnp.stack(sa_p), jnp.stack(sa_s), jnp.stack(sc_p), jnp.stack(sc_s),
            jnp.stack(v_s), mk_heads.reshape(depth, n_p, N_MEM, MEM_HEADS, MEM_HEAD_DIM),
            mv_heads.reshape(depth, n_p, N_MEM, MEM_HEADS, MEM_HEAD_DIM))
```
